```python
import math
import jax, jax.numpy as jnp
from jax import lax
import numpy as np

D_MODEL = 1024
BATCH = 16
SEQ = 2048
DEPTH = 2

MIX_WIDTH = 384
N_BRANCH = 3
S5_GROUP = 16
S5_GROUPS = MIX_WIDTH // S5_GROUP
S5_STATE = 64
S5_DT_MIN = 1e-3
S5_DT_MAX = 1e-1
SB_HEAD_DIM = 64
SB_HEADS = MIX_WIDTH // SB_HEAD_DIM
Q_BLOCK = 128
LRU_BLOCKS = 6
LRU_BLOCK = MIX_WIDTH // LRU_BLOCKS
CONV_WIDTH = 4
LRU_C = 8.0
IN_WIDTH = 6 * MIX_WIDTH + N_BRANCH * D_MODEL
SPLITS = [MIX_WIDTH * i for i in range(1, 7)]
D_FF = 2816
N_EXPERTS = 8
TOP_K = 2
D_FF_EXPERT = 3584
N_DENSE = (DEPTH + 1) // 2
N_MOE = DEPTH // 2
RMS_EPS = 1e-6

kernel_name = 'hybrid_s5_stickbreak_rglru_moe'


def rms_norm(x, g):
    xf = x.astype(jnp.float32)
    y = xf * lax.rsqrt(jnp.mean(xf * xf, axis=-1, keepdims=True) + RMS_EPS)
    return (y * g.astype(jnp.float32)).astype(x.dtype)


def _cmul(ar, ai, br, bi):
    return ar * br - ai * bi, ar * bi + ai * br


def s5_mixer(u, lam_re, lam_im, log_dt, b_re, b_im, c_re, c_im, d, w_glu, b_glu):
    Bsz, T, W = u.shape
    f32 = jnp.float32
    lam_re = lam_re.astype(f32)
    lam_im = lam_im.astype(f32)
    dt = jnp.exp(log_dt.astype(f32))[:, None]
    ea = jnp.exp(lam_re * dt)
    abar_re = ea * jnp.cos(lam_im * dt)
    abar_im = ea * jnp.sin(lam_im * dt)
    den = lam_re * lam_re + lam_im * lam_im
    nr = abar_re - 1.0
    coef_re = (nr * lam_re + abar_im * lam_im) / den
    coef_im = (abar_im * lam_re - nr * lam_im) / den
    bbar_re, bbar_im = _cmul(coef_re[..., None], coef_im[..., None],
                             b_re.astype(f32), b_im.astype(f32))
    ug = u.astype(f32).reshape(Bsz, T, S5_GROUPS, S5_GROUP)
    bu_re = jnp.einsum('btgh,gph->btgp', ug, bbar_re)
    bu_im = jnp.einsum('btgh,gph->btgp', ug, bbar_im)
    a_re = jnp.broadcast_to(abar_re, bu_re.shape)
    a_im = jnp.broadcast_to(abar_im, bu_im.shape)

    def combine(e1, e2):
        a1r, a1i, b1r, b1i = e1
        a2r, a2i, b2r, b2i = e2
        ar, ai = _cmul(a2r, a2i, a1r, a1i)
        br, bi = _cmul(a2r, a2i, b1r, b1i)
        return ar, ai, br + b2r, bi + b2i

    _, _, s_re, s_im = lax.associative_scan(combine, (a_re, a_im, bu_re, bu_im), axis=1)
    y = (jnp.einsum('btgp,ghp->btgh', s_re, c_re.astype(f32))
         - jnp.einsum('btgp,ghp->btgh', s_im, c_im.astype(f32)))
    y = y.reshape(Bsz, T, W) + d.astype(f32) * u.astype(f32)
    y = jax.nn.gelu(y).astype(u.dtype)
    return y * jax.nn.sigmoid(y @ w_glu + b_glu)


def stick_breaking_attention(q, k, v):
    Bsz, T, H, Dh = q.shape
    nb = T // Q_BLOCK
    scale = Dh ** -0.5
    qb = q.reshape(Bsz, nb, Q_BLOCK, H, Dh).transpose(1, 0, 2, 3, 4)
    key_pos = jnp.arange(T)

    def block(args):
        qi, i = args
        z = jnp.einsum('bqhd,bkhd->bhqk', qi, k).astype(jnp.float32) * scale
        q_pos = i * Q_BLOCK + jnp.arange(Q_BLOCK)
        causal = key_pos[None, :] < q_pos[:, None]
        log_beta = jax.nn.log_sigmoid(z)
        log_1m = jnp.where(causal, jax.nn.log_sigmoid(-z), 0.0)
        rest = lax.cumsum(log_1m, axis=3, reverse=True) - log_1m
        w = jnp.where(causal, jnp.exp(log_beta + rest), 0.0)
        return jnp.einsum('bhqk,bkhd->bqhd', w.astype(v.dtype), v)

    out = lax.map(block, (qb, jnp.arange(nb)))
    return out.transpose(1, 0, 2, 3, 4).reshape(Bsz, T, H * Dh)


def rglru_branch(xb, yb, conv_w, conv_b, w_a, b_a, w_x, b_x, lam):
    Bsz, T, W = xb.shape
    xc = lax.conv_general_dilated(xb, conv_w[:, None, :], window_strides=(1,),
                                  padding=[(CONV_WIDTH - 1, 0)],
                                  dimension_numbers=('NWC', 'WIO', 'NWC'),
                                  feature_group_count=W) + conv_b
    xh = xc.reshape(Bsz, T, LRU_BLOCKS, LRU_BLOCK)
    r = jax.nn.sigmoid(jnp.einsum('bthi,hij->bthj', xh, w_a).reshape(Bsz, T, W) + b_a)
    ig = jax.nn.sigmoid(jnp.einsum('bthi,hij->bthj', xh, w_x).reshape(Bsz, T, W) + b_x)
    log_a = LRU_C * r.astype(jnp.float32) * jax.nn.log_sigmoid(lam.astype(jnp.float32))
    a = jnp.exp(log_a)
    b = jnp.sqrt(-jnp.expm1(2.0 * log_a)) * (ig * xc).astype(jnp.float32)

    def combine(e1, e2):
        a1, b1 = e1
        a2, b2 = e2
        return a1 * a2, a2 * b1 + b2

    _, h = lax.associative_scan(combine, (a, b), axis=1)
    return h.astype(xb.dtype) * jax.nn.gelu(yb)


def swiglu(h, w_gate, w_up, w_down):
    return (jax.nn.silu(h @ w_gate) * (h @ w_up)) @ w_down


def moe_swiglu(h, router_w, router_b, w_gate, w_up, w_down):
    Bsz, T, D = h.shape
    hf = h.reshape(-1, D)
    logits = (hf @ router_w).astype(jnp.float32) + router_b.astype(jnp.float32)
    top_vals, top_idx = lax.top_k(logits, TOP_K)
    top_w = jax.nn.softmax(top_vals, axis=-1)
    comb = jnp.sum(jax.nn.one_hot(top_idx, N_EXPERTS, dtype=jnp.float32) * top_w[..., None],
                   axis=1).astype(h.dtype)
    out = jnp.zeros_like(hf)
    for e in range(N_EXPERTS):
        out = out + comb[:, e:e + 1] * swiglu(hf, w_gate[e], w_up[e], w_down[e])
    return out.reshape(Bsz, T, D)


def setup_inputs(seed: int = 0) -> dict:
    key = jax.random.key(seed)
    ks = iter(jax.random.split(key, 40))
    f32 = jnp.float32

    def nrm(shape, scale):
        return jax.random.normal(next(ks), shape, f32) * scale

    W, G, P, H = MIX_WIDTH, S5_GROUPS, S5_STATE, S5_GROUP
    a8 = jax.random.uniform(next(ks), (DEPTH, W), f32, 0.9, 0.999)
    a_lru = a8 ** (1.0 / LRU_C)
    return {
        'x': nrm((BATCH, SEQ, D_MODEL), 1.0),
        'mix_norm_g': 1.0 + nrm((DEPTH, D_MODEL), 0.02),
        'w_in': nrm((DEPTH, D_MODEL, IN_WIDTH), D_MODEL ** -0.5),
        'gate_b': nrm((DEPTH, N_BRANCH * D_MODEL), 0.02),
        's5_lambda_re': -0.5 + nrm((DEPTH, G, P), 0.01),
        's5_lambda_im': jnp.pi * jnp.arange(P, dtype=f32) + nrm((DEPTH, G, P), 0.01),
        's5_log_dt': jax.random.uniform(next(ks), (DEPTH, G), f32,
                                        math.log(S5_DT_MIN), math.log(S5_DT_MAX)),
        's5_b_re': nrm((DEPTH, G, P, H), (2.0 * H) ** -0.5),
        's5_b_im': nrm((DEPTH, G, P, H), (2.0 * H) ** -0.5),
        's5_c_re': nrm((DEPTH, G, H, P), P ** -0.5),
        's5_c_im': nrm((DEPTH, G, H, P), P ** -0.5),
        's5_d': nrm((DEPTH, W), 1.0),
        's5_w_glu': nrm((DEPTH, W, W), W ** -0.5),
        's5_b_glu': nrm((DEPTH, W), 0.02),
        'conv_w': nrm((DEPTH, CONV_WIDTH, W), CONV_WIDTH ** -0.5),
        'conv_b': nrm((DEPTH, W), 0.02),
        'lru_w_a': nrm((DEPTH, LRU_BLOCKS, LRU_BLOCK, LRU_BLOCK), LRU_BLOCK ** -0.5),
        'lru_b_a': nrm((DEPTH, W), 0.02),
        'lru_w_x': nrm((DEPTH, LRU_BLOCKS, LRU_BLOCK, LRU_BLOCK), LRU_BLOCK ** -0.5),
        'lru_b_x': nrm((DEPTH, W), 0.02),
        'lru_lambda': jnp.log(a_lru) - jnp.log1p(-a_lru),
        'w_branch': nrm((DEPTH, N_BRANCH, W, D_MODEL), W ** -0.5),
        'w_out': nrm((DEPTH, D_MODEL, D_MODEL), D_MODEL ** -0.5),
        'ffn_norm_g': 1.0 + nrm((DEPTH, D_MODEL), 0.02),
        'ffn_w_gate': nrm((N_DENSE, D_MODEL, D_FF), D_MODEL ** -0.5),
        'ffn_w_up': nrm((N_DENSE, D_MODEL, D_FF), D_MODEL ** -0.5),
        'ffn_w_down': nrm((N_DENSE, D_FF, D_MODEL), D_FF ** -0.5),
        'router_w': nrm((N_MOE, D_MODEL, N_EXPERTS), D_MODEL ** -0.5),
        'router_b': nrm((N_MOE, N_EXPERTS), 0.01),
        'moe_w_gate': nrm((N_MOE, N_EXPERTS, D_MODEL, D_FF_EXPERT), D_MODEL ** -0.5),
        'moe_w_up': nrm((N_MOE, N_EXPERTS, D_MODEL, D_FF_EXPERT), D_MODEL ** -0.5),
        'moe_w_down': nrm((N_MOE, N_EXPERTS, D_FF_EXPERT, D_MODEL), D_FF_EXPERT ** -0.5),
        'final_norm_g': 1.0 + nrm((D_MODEL,), 0.02),
    }


def reference(x, mix_norm_g, w_in, gate_b, s5_lambda_re, s5_lambda_im, s5_log_dt,
              s5_b_re, s5_b_im, s5_c_re, s5_c_im, s5_d, s5_w_glu, s5_b_glu,
              conv_w, conv_b, lru_w_a, lru_b_a, lru_w_x, lru_b_x, lru_lambda,
              w_branch, w_out, ffn_norm_g, ffn_w_gate, ffn_w_up, ffn_w_down,
              router_w, router_b, moe_w_gate, moe_w_up, moe_w_down, final_norm_g):
    Bsz, T, _ = x.shape
    for layer in range(DEPTH):
        h = rms_norm(x, mix_norm_g[layer])
        proj = h @ w_in[layer]
        u_s5, q, k, v, x_lru, y_lru, g = jnp.split(proj, SPLITS, axis=-1)
        a_out = s5_mixer(u_s5, s5_lambda_re[layer], s5_lambda_im[layer], s5_log_dt[layer],
                         s5_b_re[layer], s5_b_im[layer], s5_c_re[layer], s5_c_im[layer],
                         s5_d[layer], s5_w_glu[layer], s5_b_glu[layer])
        hs = (Bsz, T, SB_HEADS, SB_HEAD_DIM)
        b_out = stick_breaking_attention(q.reshape(hs), k.reshape(hs), v.reshape(hs))
        c_out = rglru_branch(x_lru, y_lru, conv_w[layer], conv_b[layer], lru_w_a[layer],
                             lru_b_a[layer], lru_w_x[layer], lru_b_x[layer], lru_lambda[layer])
        branches = jnp.stack([a_out, b_out, c_out], axis=2)
        br_d = jnp.einsum('btnw,nwd->btnd', branches, w_branch[layer])
        gates = jax.nn.sigmoid(g + gate_b[layer]).reshape(Bsz, T, N_BRANCH, D_MODEL)
        merged = jnp.sum(gates * br_d, axis=2)
        x = x + merged @ w_out[layer]
        h = rms_norm(x, ffn_norm_g[layer])
        if layer % 2 == 0:
            j = layer // 2
            x = x + swiglu(h, ffn_w_gate[j], ffn_w_up[j], ffn_w_down[j])
        else:
            j = layer // 2
            x = x + moe_swiglu(h, router_w[j], router_b[j], moe_w_gate[j],
                               moe_w_up[j], moe_w_down[j])
    return rms_norm(x, final_norm_g)
```

```python
import functools
import math

import jax
import jax.numpy as jnp
from jax import lax
from jax.experimental import pallas as pl
from jax.experimental.pallas import tpu as pltpu

F32 = jnp.float32
BF16 = jnp.bfloat16

RMS_EPS = 1e-6
MIX_WIDTH = 384
S5_GROUP = 16
S5_STATE = 64
S5_CHUNK = 64
SB_HEAD_DIM = 64
LRU_BLOCKS = 6
CONV_WIDTH = 4
LRU_C = 8.0
N_BRANCH = 3
TOP_K = 2
LANES = 128
NEG_BIG = -1e30
VMEM_LIMIT = 56 * 1024 * 1024


def _cparams(*sem):
    return pltpu.CompilerParams(dimension_semantics=sem, vmem_limit_bytes=VMEM_LIMIT)


def _resident(shape):
    n = len(shape)
    return pl.BlockSpec(shape, lambda *_: (0,) * n, pipeline_mode=pl.Buffered(1))


def _rms(xf, g):
    return xf * lax.rsqrt(jnp.mean(xf * xf, axis=-1, keepdims=True) + RMS_EPS) * g


def _gelu(x):
    c = math.sqrt(2.0 / math.pi)
    return 0.5 * x * (1.0 + jnp.tanh(c * (x + 0.044715 * (x * x * x))))


def _sigmoid(x):
    return 1.0 / (1.0 + jnp.exp(-x))


def _dot(a, b):
    return jnp.dot(a, b, preferred_element_type=F32)


def _norm_proj_kernel(x_ref, g_ref, w_ref, main_ref, xl_ref, yl_ref, gate_ref, *, w):
    h = _rms(x_ref[...], g_ref[...]).astype(BF16)
    main_ref[...] = _dot(h, w_ref[:, 0:4 * w]).astype(BF16)
    xl_ref[...] = _dot(h, w_ref[:, 4 * w:5 * w]).astype(BF16)
    yl_ref[...] = _dot(h, w_ref[:, 5 * w:6 * w]).astype(BF16)
    d = gate_ref.shape[-1] // N_BRANCH
    for n in range(N_BRANCH):
        c0 = 6 * w + n * d
        gate_ref[:, n * d:(n + 1) * d] = _dot(h, w_ref[:, c0:c0 + d]).astype(BF16)


def _norm_proj(x, g, w_in, tm):
    B, T, D = x.shape
    W = MIX_WIDTH
    return pl.pallas_call(
        functools.partial(_norm_proj_kernel, w=W),
        grid=(B, T // tm),
        in_specs=[pl.BlockSpec((None, tm, D), lambda b, i: (b, i, 0)),
                  _resident((1, D)),
                  _resident(w_in.shape)],
        out_specs=[pl.BlockSpec((None, tm, 4 * W), lambda b, i: (b, i, 0)),
                   pl.BlockSpec((tm, W), lambda b, i: (i, b)),
                   pl.BlockSpec((tm, W), lambda b, i: (i, b)),
                   pl.BlockSpec((None, tm, N_BRANCH * D), lambda b, i: (b, i, 0))],
        out_shape=[jax.ShapeDtypeStruct((B, T, 4 * W), BF16),
                   jax.ShapeDtypeStruct((T, B * W), BF16),
                   jax.ShapeDtypeStruct((T, B * W), BF16),
                   jax.ShapeDtypeStruct((B, T, N_BRANCH * D), BF16)],
        compiler_params=_cparams("parallel", "parallel"),
        name="norm_proj",
    )(x, g.reshape(1, D), w_in)


def _s5_matrices(lam_re, lam_im, log_dt, b_re, b_im, c_re, c_im, L):
    hp = lax.Precision.HIGHEST
    G, P, H = b_re.shape
    dt = jnp.exp(log_dt)[:, None]
    ea = jnp.exp(lam_re * dt)
    abar_re = ea * jnp.cos(lam_im * dt)
    abar_im = ea * jnp.sin(lam_im * dt)
    den = lam_re * lam_re + lam_im * lam_im
    nr = abar_re - 1.0
    coef_re = (nr * lam_re + abar_im * lam_im) / den
    coef_im = (abar_im * lam_re - nr * lam_im) / den
    bb_re = coef_re[..., None] * b_re - coef_im[..., None] * b_im
    bb_im = coef_re[..., None] * b_im + coef_im[..., None] * b_re
    k = jnp.arange(L + 1, dtype=F32)
    mag = jnp.exp((lam_re * dt)[..., None] * k)
    ang = (lam_im * dt)[..., None] * k
    pw_re = mag * jnp.cos(ang)
    pw_im = mag * jnp.sin(ang)
    pb_re = pw_re[..., None] * bb_re[:, :, None, :] - pw_im[..., None] * bb_im[:, :, None, :]
    pb_im = pw_re[..., None] * bb_im[:, :, None, :] + pw_im[..., None] * bb_re[:, :, None, :]
    klag = (jnp.einsum('gpkh,gop->gkho', pb_re[:, :, :L], c_re, precision=hp)
            - jnp.einsum('gpkh,gop->gkho', pb_im[:, :, :L], c_im, precision=hp))
    lin = jnp.arange(L)[:, None]
    lout = jnp.arange(L)[None, :]
    lag = lout - lin
    toe = klag[:, jnp.maximum(lag, 0)]
    toe = jnp.where((lag >= 0)[None, :, :, None, None], toe, 0.0)
    k_intra = toe.transpose(0, 1, 3, 2, 4).reshape(G, L * H, L * H)
    st_re = pb_re[:, :, :L][:, :, ::-1].transpose(0, 2, 3, 1)
    st_im = pb_im[:, :, :L][:, :, ::-1].transpose(0, 2, 3, 1)
    k_state = jnp.concatenate([st_re, st_im], axis=-1).reshape(G, L * H, 2 * P)
    ca_re = (c_re[:, :, :, None] * pw_re[:, None, :, 1:] - c_im[:, :, :, None] * pw_im[:, None, :, 1:])
    ca_im = (c_re[:, :, :, None] * pw_im[:, None, :, 1:] + c_im[:, :, :, None] * pw_re[:, None, :, 1:])
    k_out = jnp.concatenate([ca_re.transpose(0, 2, 3, 1), -ca_im.transpose(0, 2, 3, 1)],
                            axis=1).reshape(G, 2 * P, L * H)
    al_re = pw_re[:, :, L]
    al_im = pw_im[:, :, L]
    dec_same = jnp.concatenate([al_re, al_re], axis=-1)[:, None, :]
    dec_swap = jnp.concatenate([-al_im, al_im], axis=-1)[:, None, :]
    return k_intra.astype(BF16), k_state.astype(BF16), k_out.astype(BF16), dec_same, dec_swap


def _s5_kernel(x_ref, ki_ref, ks_ref, ko_ref, dsame_ref, dswap_ref, y_ref, s_scr, *, nc, nb, p):
    x = x_ref[...]
    ds = _dot(x, ks_ref[...])
    dsame = dsame_ref[...]
    dswap = dswap_ref[...]
    s = jnp.zeros((nb, 2 * p), F32)
    for c in range(nc):
        s_scr[c * nb:(c + 1) * nb, :] = s
        s = dsame * s + dswap * pltpu.roll(s, p, axis=1) + ds[c * nb:(c + 1) * nb, :]
    y = _dot(x, ki_ref[...]) + _dot(s_scr[...].astype(BF16), ko_ref[...])
    y_ref[...] = y.astype(BF16)


def _s5_scan(u, mats, L):
    k_intra, k_state, k_out, dsame, dswap = mats
    B, T, W = u.shape
    H, P = S5_GROUP, S5_STATE
    G = W // H
    nc = T // L
    M = nc * B
    xg = u.reshape(B, nc, L, G, H).transpose(3, 1, 0, 2, 4).reshape(G, M, L * H)
    y = pl.pallas_call(
        functools.partial(_s5_kernel, nc=nc, nb=B, p=P),
        grid=(G,),
        in_specs=[pl.BlockSpec((None, M, L * H), lambda g: (g, 0, 0)),
                  pl.BlockSpec((None, L * H, L * H), lambda g: (g, 0, 0)),
                  pl.BlockSpec((None, L * H, 2 * P), lambda g: (g, 0, 0)),
                  pl.BlockSpec((None, 2 * P, L * H), lambda g: (g, 0, 0)),
                  pl.BlockSpec((None, 1, 2 * P), lambda g: (g, 0, 0)),
                  pl.BlockSpec((None, 1, 2 * P), lambda g: (g, 0, 0))],
        out_specs=pl.BlockSpec((None, M, L * H), lambda g: (g, 0, 0)),
        out_shape=jax.ShapeDtypeStruct((G, M, L * H), BF16),
        scratch_shapes=[pltpu.VMEM((M, 2 * P), F32)],
        compiler_params=_cparams("parallel"),
        name="s5_scan",
    )(xg, k_intra, k_state, k_out, dsame, dswap)
    return y.reshape(G, nc, B, L, H).transpose(2, 1, 3, 0, 4).reshape(B, T, W)


def _attn_kernel(q_ref, k_ref, v_ref, o_ref, *, tq, heads, dh):
    i = pl.program_id(1)
    scale = dh ** -0.5
    row = lax.broadcasted_iota(jnp.int32, (tq, tq), 0)
    col = lax.broadcasted_iota(jnp.int32, (tq, tq), 1)
    suffix = (row > col).astype(BF16)
    causal = col < row
    nt = (((1,), (1,)), ((), ()))

    def scores(q, r0, hs):
        z = lax.dot_general(q, k_ref[pl.ds(r0, tq), hs], nt, preferred_element_type=F32)
        sp = jnp.log(1.0 + jnp.exp(-jnp.abs(z)))
        log_beta = jnp.minimum(z, 0.0) - sp
        return log_beta, log_beta - z

    for h in range(heads):
        hs = slice(h * dh, (h + 1) * dh)
        q = (q_ref[:, hs].astype(F32) * scale).astype(BF16)
        r0 = pl.multiple_of(i * tq, tq)
        log_beta, log_1m = scores(q, r0, hs)
        log_1m = jnp.where(causal, log_1m, 0.0)
        cum = _dot(log_1m.astype(BF16), suffix)
        wgt = jnp.where(causal, jnp.exp(log_beta + cum), 0.0)
        acc = _dot(wgt.astype(BF16), v_ref[pl.ds(r0, tq), hs])
        run = cum[:, 0:1] + log_1m[:, 0:1]

        def body(j, carry, q=q, hs=hs):
            run, acc = carry
            r0 = pl.multiple_of((i - 1 - j) * tq, tq)
            log_beta, log_1m = scores(q, r0, hs)
            cum = _dot(log_1m.astype(BF16), suffix)
            wgt = jnp.exp(log_beta + cum + run)
            acc = acc + _dot(wgt.astype(BF16), v_ref[pl.ds(r0, tq), hs])
            return run + cum[:, 0:1] + log_1m[:, 0:1], acc

        run, acc = lax.fori_loop(0, i, body, (run, acc))
        o_ref[:, hs] = acc.astype(BF16)


def _attention(main, tq):
    B, T, W4 = main.shape
    W = W4 // 4
    heads = W // SB_HEAD_DIM
    return pl.pallas_call(
        functools.partial(_attn_kernel, tq=tq, heads=heads, dh=SB_HEAD_DIM),
        grid=(B, T // tq),
        in_specs=[pl.BlockSpec((None, tq, W), lambda b, i: (b, i, 1)),
                  pl.BlockSpec((None, T, W), lambda b, i: (b, 0, 2)),
                  pl.BlockSpec((None, T, W), lambda b, i: (b, 0, 3))],
        out_specs=pl.BlockSpec((None, tq, W), lambda b, i: (b, i, 0)),
        out_shape=jax.ShapeDtypeStruct((B, T, W), BF16),
        compiler_params=_cparams("parallel", "arbitrary"),
        name="sb_attention",
    )(main, main, main)


def _lru_kernel(x_ref, y_ref, cw_ref, cb_ref, wa_ref, ba_ref, wx_ref, bx_ref, lam_ref, o_ref,
                tail_scr, h_scr, a_scr, b_scr, *, tt, nb):
    rows = tt * nb
    halo = (CONV_WIDTH - 1) * nb

    @pl.when(pl.program_id(0) == 0)
    def _():
        tail_scr[...] = jnp.zeros_like(tail_scr)
        h_scr[...] = jnp.zeros_like(h_scr)

    x = x_ref[...].astype(F32)
    xx = jnp.concatenate([tail_scr[...], x], axis=0)
    tail_scr[...] = x[rows - halo:, :]
    xc = cb_ref[...] + cw_ref[0:1, :] * xx[0:rows, :]
    for j in range(1, CONV_WIDTH):
        xc = xc + cw_ref[j:j + 1, :] * xx[j * nb:j * nb + rows, :]
    xcb = xc.astype(BF16)
    r = _sigmoid(_dot(xcb, wa_ref[...]) + ba_ref[...])
    ig = _sigmoid(_dot(xcb, wx_ref[...]) + bx_ref[...])
    lam = lam_ref[...]
    log_sig_lam = jnp.minimum(lam, 0.0) - jnp.log(1.0 + jnp.exp(-jnp.abs(lam)))
    log_a = LRU_C * r * log_sig_lam
    a_scr[...] = jnp.exp(log_a)
    b_scr[...] = jnp.sqrt(1.0 - jnp.exp(2.0 * log_a)) * (ig * xc)

    def step(t, h):
        r0 = pl.multiple_of(t * nb, nb)
        h = a_scr[pl.ds(r0, nb), :] * h + b_scr[pl.ds(r0, nb), :]
        b_scr[pl.ds(r0, nb), :] = h
        return h

    h_scr[...] = lax.fori_loop(0, tt, step, h_scr[...], unroll=8)
    o_ref[...] = (b_scr[...] * _gelu(y_ref[...].astype(F32))).astype(BF16)


def _block_diag(w):
    n, k, _ = w.shape
    eye = jnp.eye(n, dtype=w.dtype)
    return (eye[:, None, :, None] * w[:, :, None, :]).reshape(n * k, n * k)


def _rglru(x_tm, y_tm, conv_w, conv_b, w_a, b_a, w_x, b_x, lam, nb, tt):
    R, W = x_tm.shape
    T = R // nb
    rows = tt * nb
    row = lambda v: v.reshape(1, W)
    return pl.pallas_call(
        functools.partial(_lru_kernel, tt=tt, nb=nb),
        grid=(T // tt,),
        in_specs=[pl.BlockSpec((rows, W), lambda i: (i, 0)),
                  pl.BlockSpec((rows, W), lambda i: (i, 0)),
                  _resident((CONV_WIDTH, W)), _resident((1, W)),
                  _resident((W, W)), _resident((1, W)),
                  _resident((W, W)), _resident((1, W)), _resident((1, W))],
        out_specs=pl.BlockSpec((rows, W), lambda i: (i, 0)),
        out_shape=jax.ShapeDtypeStruct((R, W), BF16),
        scratch_shapes=[pltpu.VMEM(((CONV_WIDTH - 1) * nb, W), F32),
                        pltpu.VMEM((nb, W), F32),
                        pltpu.VMEM((rows, W), F32),
                        pltpu.VMEM((rows, W), F32)],
        compiler_params=_cparams("arbitrary"),
        name="rglru",
    )(x_tm, y_tm, conv_w, row(conv_b), _block_diag(w_a).astype(BF16), row(b_a),
      _block_diag(w_x).astype(BF16), row(b_x), row(lam))


def _merge_kernel(u_ref, ys_ref, at_ref, lr_ref, gate_ref, x_ref, d_ref, wglu_ref, bglu_ref,
                  wbr_ref, gb_ref, wout_ref, o_ref):
    dm = x_ref.shape[-1]
    ya = _gelu(ys_ref[...].astype(F32) + d_ref[...] * u_ref[...].astype(F32))
    a_out = ya * _sigmoid(_dot(ya.astype(BF16), wglu_ref[...]) + bglu_ref[...])
    branches = (a_out.astype(BF16), at_ref[...], lr_ref[...])
    merged = None
    for n in range(N_BRANCH):
        gate = _sigmoid(gate_ref[:, n * dm:(n + 1) * dm].astype(F32) + gb_ref[:, n * dm:(n + 1) * dm])
        term = gate * _dot(branches[n], wbr_ref[n])
        merged = term if merged is None else merged + term
    o_ref[...] = x_ref[...] + _dot(merged.astype(BF16), wout_ref[...])


def _merge(main, ys, attn, lru_tm, gates, x, d, w_glu, b_glu, w_branch, gate_b, w_out, tm):
    B, T, D = x.shape
    W = MIX_WIDTH
    tok = lambda width, c=0: pl.BlockSpec((None, tm, width), lambda b, i: (b, i, c))
    return pl.pallas_call(
        _merge_kernel,
        grid=(B, T // tm),
        in_specs=[tok(W), tok(W), tok(W),
                  pl.BlockSpec((tm, W), lambda b, i: (i, b)),
                  tok(N_BRANCH * D), tok(D),
                  _resident((1, W)), _resident((W, W)), _resident((1, W)),
                  _resident((N_BRANCH, W, D)), _resident((1, N_BRANCH * D)), _resident((D, D))],
        out_specs=tok(D),
        out_shape=jax.ShapeDtypeStruct((B, T, D), F32),
        compiler_params=_cparams("parallel", "parallel"),
        name="merge_out",
    )(main, ys, attn, lru_tm, gates, x, d.reshape(1, W), w_glu, b_glu.reshape(1, W),
      w_branch, gate_b.reshape(1, N_BRANCH * D), w_out)


def _ffn_kernel(x_ref, g_ref, wg_ref, wu_ref, wd_ref, o_ref, *, tf):
    x = x_ref[...]
    h = _rms(x, g_ref[...]).astype(BF16)
    acc = x
    for c0 in range(0, wg_ref.shape[1], tf):
        gate = _dot(h, wg_ref[:, c0:c0 + tf])
        up = _dot(h, wu_ref[:, c0:c0 + tf])
        act = (gate * _sigmoid(gate) * up).astype(BF16)
        acc = acc + _dot(act, wd_ref[c0:c0 + tf, :])
    o_ref[...] = acc


def _ffn(x2, g, w_gate, w_up, w_down, tm, tf):
    N, D = x2.shape
    F = w_gate.shape[1]
    return pl.pallas_call(
        functools.partial(_ffn_kernel, tf=tf),
        grid=(N // tm,),
        in_specs=[pl.BlockSpec((tm, D), lambda i: (i, 0)), _resident((1, D)),
                  _resident((D, F)), _resident((D, F)), _resident((F, D))],
        out_specs=pl.BlockSpec((tm, D), lambda i: (i, 0)),
        out_shape=jax.ShapeDtypeStruct((N, D), F32),
        compiler_params=_cparams("parallel"),
        name="ffn_swiglu",
    )(x2, g.reshape(1, D), w_gate, w_up, w_down)


def _router_kernel(x_ref, g_ref, rw_ref, rb_ref, h_ref, idx_ref, wgt_ref, *, n_exp):
    h = _rms(x_ref[...], g_ref[...])
    h_ref[...] = h.astype(BF16)
    logits = jnp.dot(h, rw_ref[...], preferred_element_type=F32, precision=lax.Precision.HIGHEST)
    lane = lax.broadcasted_iota(jnp.int32, logits.shape, 1)
    lg = jnp.where(lane < n_exp, logits + rb_ref[...], NEG_BIG)
    m1 = jnp.max(lg, axis=1, keepdims=True)
    i1 = jnp.min(jnp.where(lg == m1, lane, LANES), axis=1, keepdims=True)
    lg2 = jnp.where(lane == i1, NEG_BIG, lg)
    m2 = jnp.max(lg2, axis=1, keepdims=True)
    i2 = jnp.min(jnp.where(lg2 == m2, lane, LANES), axis=1, keepdims=True)
    e = jnp.exp(m2 - m1)
    w1 = 1.0 / (1.0 + e)
    w2 = e / (1.0 + e)
    idx_ref[...] = jnp.where(lane == 0, i1, jnp.where(lane == 1, i2, 0))
    wgt_ref[...] = jnp.where(lane == 0, w1, jnp.where(lane == 1, w2, 0.0))


def _router(x2, g, router_w, router_b, tm):
    N, D = x2.shape
    E = router_w.shape[1]
    rw = jnp.zeros((D, LANES), F32).at[:, :E].set(router_w)
    rb = jnp.zeros((1, LANES), F32).at[0, :E].set(router_b)
    return pl.pallas_call(
        functools.partial(_router_kernel, n_exp=E),
        grid=(N // tm,),
        in_specs=[pl.BlockSpec((tm, D), lambda i: (i, 0)), _resident((1, D)),
                  _resident((D, LANES)), _resident((1, LANES))],
        out_specs=[pl.BlockSpec((tm, D), lambda i: (i, 0)),
                   pl.BlockSpec((tm, LANES), lambda i: (i, 0)),
                   pl.BlockSpec((tm, LANES), lambda i: (i, 0))],
        out_shape=[jax.ShapeDtypeStruct((N, D), BF16),
                   jax.ShapeDtypeStruct((N, LANES), jnp.int32),
                   jax.ShapeDtypeStruct((N, LANES), F32)],
        compiler_params=_cparams("parallel"),
        name="moe_router",
    )(x2, g.reshape(1, D), rw, rb)


def _expert_kernel(te_ref, nu_ref, xs_ref, wg_ref, wu_ref, wd_ref, o_ref, acc_ref):
    t = pl.program_id(0)
    f = pl.program_id(1)
    used = t < nu_ref[0]

    @pl.when(f == 0)
    def _():
        acc_ref[...] = jnp.zeros_like(acc_ref)

    @pl.when(used)
    def _():
        xs = xs_ref[...]
        gate = _dot(xs, wg_ref[...])
        up = _dot(xs, wu_ref[...])
        act = (gate * _sigmoid(gate) * up).astype(BF16)
        acc_ref[...] += _dot(act, wd_ref[...])

    @pl.when(f == pl.num_programs(1) - 1)
    def _():
        o_ref[...] = acc_ref[...].astype(BF16)


def _experts(xs, tile_expert, n_used, w_gate, w_up, w_down, tm, tf):
    S, D = xs.shape
    F = w_gate.shape[2]
    return pl.pallas_call(
        _expert_kernel,
        grid_spec=pltpu.PrefetchScalarGridSpec(
            num_scalar_prefetch=2,
            grid=(S // tm, F // tf),
            in_specs=[pl.BlockSpec((tm, D), lambda t, f, te, nu: (t, 0)),
                      pl.BlockSpec((None, D, tf), lambda t, f, te, nu: (te[t], 0, f)),
                      pl.BlockSpec((None, D, tf), lambda t, f, te, nu: (te[t], 0, f)),
                      pl.BlockSpec((None, tf, D), lambda t, f, te, nu: (te[t], f, 0))],
            out_specs=pl.BlockSpec((tm, D), lambda t, f, te, nu: (t, 0)),
            scratch_shapes=[pltpu.VMEM((tm, D), F32)]),
        out_shape=jax.ShapeDtypeStruct((S, D), BF16),
        compiler_params=_cparams("parallel", "arbitrary"),
        name="moe_experts",
    )(tile_expert, n_used, xs, w_gate, w_up, w_down)


def _route(idx, n_exp, tm):
    N, K = idx.shape
    flat = idx.reshape(-1)
    onehot = (flat[:, None] == jnp.arange(n_exp)[None, :]).astype(jnp.int32)
    rank = jnp.cumsum(onehot, axis=0) - onehot
    counts = jnp.sum(onehot, axis=0)
    padded = ((counts + tm - 1) // tm) * tm
    ends = jnp.cumsum(padded)
    starts = ends - padded
    slot = starts[flat] + jnp.sum(rank * onehot, axis=1)
    S = N * K + n_exp * tm
    src = jnp.zeros((S,), jnp.int32).at[slot].set(jnp.arange(N * K, dtype=jnp.int32) // K)
    tile_start = jnp.arange(S // tm, dtype=jnp.int32) * tm
    tile_expert = jnp.minimum(jnp.searchsorted(ends, tile_start, side='right'), n_exp - 1)
    n_used = (ends[-1] // tm).astype(jnp.int32).reshape(1)
    last = tile_expert[jnp.maximum(n_used[0] - 1, 0)]
    tile_expert = jnp.where(tile_start < ends[-1], tile_expert, last).astype(jnp.int32)
    return src, slot.reshape(N, K), tile_expert, n_used


def _combine_kernel(x_ref, y0_ref, y1_ref, wgt_ref, g_ref, o_ref, *, final_norm):
    w0 = wgt_ref[:, 0:1]
    w1 = wgt_ref[:, 1:2]
    x = x_ref[...] + w0 * y0_ref[...].astype(F32) + w1 * y1_ref[...].astype(F32)
    o_ref[...] = _rms(x, g_ref[...]) if final_norm else x


def _combine(x2, y0, y1, wgt, g, tm, final_norm):
    N, D = x2.shape
    return pl.pallas_call(
        functools.partial(_combine_kernel, final_norm=final_norm),
        grid=(N // tm,),
        in_specs=[pl.BlockSpec((tm, D), lambda i: (i, 0)),
                  pl.BlockSpec((tm, D), lambda i: (i, 0)),
                  pl.BlockSpec((tm, D), lambda i: (i, 0)),
                  pl.BlockSpec((tm, LANES), lambda i: (i, 0)),
                  _resident((1, D))],
        out_specs=pl.BlockSpec((tm, D), lambda i: (i, 0)),
        out_shape=jax.ShapeDtypeStruct((N, D), F32),
        compiler_params=_cparams("parallel"),
        name="moe_combine",
    )(x2, y0, y1, wgt, g.reshape(1, D))


def _moe(x2, g, router_w, router_b, w_gate, w_up, w_down, final_g, tm_tok, tm_exp, tf):
    n_exp = router_w.shape[1]
    h, idx, wgt = _router(x2, g, router_w, router_b, tm_tok)
    src, slot, tile_expert, n_used = _route(idx[:, :TOP_K], n_exp, tm_exp)
    xs = jnp.take(h, src, axis=0)
    ys = _experts(xs, tile_expert, n_used, w_gate, w_up, w_down, tm_exp, tf)
    y0 = jnp.take(ys, slot[:, 0], axis=0)
    y1 = jnp.take(ys, slot[:, 1], axis=0)
    gn = final_g if final_g is not None else jnp.ones((x2.shape[1],), F32)
    return _combine(x2, y0, y1, wgt, gn, tm_tok, final_g is not None)


def _final_norm_kernel(x_ref, g_ref, o_ref):
    o_ref[...] = _rms(x_ref[...], g_ref[...])


def _final_norm(x2, g, tm):
    N, D = x2.shape
    return pl.pallas_call(
        _final_norm_kernel,
        grid=(N // tm,),
        in_specs=[pl.BlockSpec((tm, D), lambda i: (i, 0)), _resident((1, D))],
        out_specs=pl.BlockSpec((tm, D), lambda i: (i, 0)),
        out_shape=jax.ShapeDtypeStruct((N, D), F32),
        compiler_params=_cparams("parallel"),
        name="final_norm",
    )(x2, g.reshape(1, D))


def _tile(n, pref):
    t = min(n, pref)
    while n % t:
        t //= 2
    return t


def kernel(x, mix_norm_g, w_in, gate_b, s5_lambda_re, s5_lambda_im, s5_log_dt, s5_b_re, s5_b_im, s5_c_re, s5_c_im, s5_d, s5_w_glu, s5_b_glu, conv_w, conv_b, lru_w_a, lru_b_a, lru_w_x, lru_b_x, lru_lambda, w_branch, w_out, ffn_norm_g, ffn_w_gate, ffn_w_up, ffn_w_down, router_w, router_b, moe_w_gate, moe_w_up, moe_w_down, final_norm_g):
    B, T, D = x.shape
    depth = w_in.shape[0]
    W = MIX_WIDTH
    N = B * T
    tm = _tile(T, 512)
    tq = _tile(T, 256)
    tt = _tile(T, 128)
    L = _tile(T, S5_CHUNK)
    tm_exp = _tile(N, 512)
    x = x.astype(F32)
    for layer in range(depth):
        main, xl_tm, yl_tm, gates = _norm_proj(x, mix_norm_g[layer], w_in[layer].astype(BF16), tm)
        mats = _s5_matrices(s5_lambda_re[layer], s5_lambda_im[layer], s5_log_dt[layer],
                            s5_b_re[layer], s5_b_im[layer], s5_c_re[layer], s5_c_im[layer], L)
        ys = _s5_scan(main[:, :, :W], mats, L)
        attn = _attention(main, tq)
        lru = _rglru(xl_tm.reshape(T * B, W), yl_tm.reshape(T * B, W), conv_w[layer], conv_b[layer],
                     lru_w_a[layer], lru_b_a[layer], lru_w_x[layer], lru_b_x[layer],
                     lru_lambda[layer], B, tt)
        x = _merge(main, ys, attn, lru.reshape(T, B * W), gates, x, s5_d[layer],
                   s5_w_glu[layer].astype(BF16), s5_b_glu[layer], w_branch[layer].astype(BF16),
                   gate_b[layer], w_out[layer].astype(BF16), tm)
        x2 = x.reshape(N, D)
        last = layer == depth - 1
        j = layer // 2
        if layer % 2 == 0:
            x2 = _ffn(x2, ffn_norm_g[layer], ffn_w_gate[j].astype(BF16), ffn_w_up[j].astype(BF16),
                      ffn_w_down[j].astype(BF16), tm, ffn_w_gate.shape[2] // 2)
            if last:
                x2 = _final_norm(x2, final_norm_g, tm)
        else:
            x2 = _moe(x2, ffn_norm_g[layer], router_w[j], router_b[j], moe_w_gate[j].astype(BF16),
                      moe_w_up[j].astype(BF16), moe_w_down[j].astype(BF16),
                      final_norm_g if last else None, tm, tm_exp, 512)
        x = x2.reshape(B, T, D)
    return x
```

```python
import functools
import math

import jax
import jax.numpy as jnp
from jax import lax
from jax.experimental import pallas as pl
from jax.experimental.pallas import tpu as pltpu

F32 = jnp.float32
BF16 = jnp.bfloat16

RMS_EPS = 1e-6
MIX_WIDTH = 384
S5_GROUP = 16
S5_STATE = 64
S5_CHUNK = 128
SB_HEAD_DIM = 64
LRU_BLOCKS = 6
CONV_WIDTH = 4
LRU_C = 8.0
N_BRANCH = 3
TOP_K = 2
LANES = 128
NEG_BIG = -1e30
LOG2E = 1.4426950408889634
POW2_ZERO_BELOW = -150.0
VMEM_LIMIT = 56 * 1024 * 1024


def _cparams(*sem):
    return pltpu.CompilerParams(dimension_semantics=sem, vmem_limit_bytes=VMEM_LIMIT)


def _resident(shape):
    n = len(shape)
    return pl.BlockSpec(shape, lambda *_: (0,) * n, pipeline_mode=pl.Buffered(1))


def _rms(xf, g):
    return xf * lax.rsqrt(jnp.mean(xf * xf, axis=-1, keepdims=True) + RMS_EPS) * g


def _gelu(x):
    c = math.sqrt(2.0 / math.pi)
    return 0.5 * x * (1.0 + jnp.tanh(c * (x + 0.044715 * (x * x * x))))


def _sigmoid(x):
    return 1.0 / (1.0 + jnp.exp(-x))


def _dot(a, b):
    return jnp.dot(a, b, preferred_element_type=F32)


def _norm_proj_kernel(x_ref, g_ref, w_ref, main_ref, xl_ref, yl_ref, gate_ref, *, w):
    h = _rms(x_ref[...], g_ref[...]).astype(BF16)
    main_ref[...] = _dot(h, w_ref[:, 0:4 * w]).astype(BF16)
    xl_ref[...] = _dot(h, w_ref[:, 4 * w:5 * w]).astype(BF16)
    yl_ref[...] = _dot(h, w_ref[:, 5 * w:6 * w]).astype(BF16)
    d = gate_ref.shape[-1] // N_BRANCH
    for n in range(N_BRANCH):
        c0 = 6 * w + n * d
        gate_ref[:, n * d:(n + 1) * d] = _dot(h, w_ref[:, c0:c0 + d]).astype(BF16)


def _norm_proj(x, g, w_in, tm):
    B, T, D = x.shape
    W = MIX_WIDTH
    return pl.pallas_call(
        functools.partial(_norm_proj_kernel, w=W),
        grid=(B, T // tm),
        in_specs=[pl.BlockSpec((None, tm, D), lambda b, i: (b, i, 0)),
                  _resident((1, D)),
                  _resident(w_in.shape)],
        out_specs=[pl.BlockSpec((None, tm, 4 * W), lambda b, i: (b, i, 0)),
                   pl.BlockSpec((tm, W), lambda b, i: (i, b)),
                   pl.BlockSpec((tm, W), lambda b, i: (i, b)),
                   pl.BlockSpec((None, tm, N_BRANCH * D), lambda b, i: (b, i, 0))],
        out_shape=[jax.ShapeDtypeStruct((B, T, 4 * W), BF16),
                   jax.ShapeDtypeStruct((T, B * W), BF16),
                   jax.ShapeDtypeStruct((T, B * W), BF16),
                   jax.ShapeDtypeStruct((B, T, N_BRANCH * D), BF16)],
        compiler_params=_cparams("parallel", "parallel"),
        name="norm_proj",
    )(x, g.reshape(1, D), w_in)


def _s5_consts(lam_re, lam_im, log_dt, b_re, b_im, c_re, c_im, L):
    G, P, H = b_re.shape
    ns = (G * H) // LANES
    gs = G // ns
    dt = jnp.exp(log_dt)[:, None]
    ea = jnp.exp(lam_re * dt)
    abar_re = ea * jnp.cos(lam_im * dt)
    abar_im = ea * jnp.sin(lam_im * dt)
    den = lam_re * lam_re + lam_im * lam_im
    nr = abar_re - 1.0
    coef_re = (nr * lam_re + abar_im * lam_im) / den
    coef_im = (abar_im * lam_re - nr * lam_im) / den
    bb_re = coef_re[..., None] * b_re - coef_im[..., None] * b_im
    bb_im = coef_re[..., None] * b_im + coef_im[..., None] * b_re
    eye = jnp.eye(gs, dtype=F32)

    def in_slab(bb):
        return jnp.einsum('sgph,gk->sghkp', bb.reshape(ns, gs, P, H), eye).reshape(ns, gs * H, gs * P)

    def out_slab(c):
        return jnp.einsum('sgop,gk->sgpko', c.reshape(ns, gs, H, P), eye).reshape(ns, gs * P, gs * H)

    b_slab = jnp.concatenate([in_slab(bb_re), in_slab(bb_im)], axis=-1).astype(BF16)
    c_slab = jnp.concatenate([out_slab(c_re), -out_slab(c_im)], axis=1).astype(BF16)

    def power(k):
        mag = jnp.exp((lam_re * dt)[None] * k[:, None, None])
        ang = (lam_im * dt)[None] * k[:, None, None]
        lay = lambda v: v.reshape(k.shape[0], ns, gs * P).transpose(1, 0, 2)
        return lay(mag * jnp.cos(ang)), lay(mag * jnp.sin(ang))

    t = jnp.arange(L, dtype=F32)
    m = float(L // 2)
    em_re, em_im = power(m - t)
    ep_re, ep_im = power(t - m)
    vf_re, vf_im = power(jnp.full((1,), m + 1.0, F32))
    return b_slab, c_slab, em_re, em_im, ep_re, ep_im, vf_re, vf_im


def _s5_kernel(u_ref, b_ref, c_ref, emr_ref, emi_ref, epr_ref, epi_ref, vfr_ref, vfi_ref,
               d_ref, wglu_ref, bglu_ref, o_ref, carry_scr, s_scr, *, L):
    tc = u_ref.shape[0]
    ns = b_ref.shape[0]
    half = b_ref.shape[2] // 2

    @pl.when(pl.program_id(1) == 0)
    def _():
        carry_scr[...] = jnp.zeros_like(carry_scr)

    row = lax.broadcasted_iota(jnp.int32, (L, L), 0)
    col = lax.broadcasted_iota(jnp.int32, (L, L), 1)
    tri = (col <= row).astype(BF16)
    ys = []
    for j in range(ns):
        u = u_ref[:, j * LANES:(j + 1) * LANES]
        bu = _dot(u, b_ref[j])
        emr, emi, epr, epi = emr_ref[j], emi_ref[j], epr_ref[j], epi_ref[j]
        vfr, vfi = vfr_ref[j], vfi_ref[j]
        c_re = carry_scr[2 * j:2 * j + 1, :]
        c_im = carry_scr[2 * j + 1:2 * j + 2, :]
        for c in range(tc // L):
            bur = bu[c * L:(c + 1) * L, :half]
            bui = bu[c * L:(c + 1) * L, half:]
            scaled = jnp.concatenate([bur * emr - bui * emi, bur * emi + bui * emr], axis=1)
            pre = _dot(tri, scaled.astype(BF16))
            pr = pre[:, :half] + (vfr * c_re - vfi * c_im)
            pi = pre[:, half:] + (vfr * c_im + vfi * c_re)
            s_re = pr * epr - pi * epi
            s_im = pr * epi + pi * epr
            c_re = s_re[L - 1:L, :]
            c_im = s_im[L - 1:L, :]
            s_scr[c * L:(c + 1) * L, :] = jnp.concatenate([s_re, s_im], axis=1).astype(BF16)
        carry_scr[2 * j:2 * j + 1, :] = c_re
        carry_scr[2 * j + 1:2 * j + 2, :] = c_im
        ys.append(_dot(s_scr[...], c_ref[j]))
    y = jnp.concatenate(ys, axis=1) + d_ref[...] * u_ref[...].astype(F32)
    ya = _gelu(y)
    o_ref[...] = (ya * _sigmoid(_dot(ya.astype(BF16), wglu_ref[...]) + bglu_ref[...])).astype(BF16)


def _s5(main, consts, d, w_glu, b_glu, tc, L):
    B, T, _ = main.shape
    W = MIX_WIDTH
    b_slab, c_slab = consts[0], consts[1]
    ns, _, two_half = b_slab.shape
    return pl.pallas_call(
        functools.partial(_s5_kernel, L=L),
        grid=(B, T // tc),
        in_specs=[pl.BlockSpec((None, tc, W), lambda b, i: (b, i, 0))]
        + [_resident(c.shape) for c in consts]
        + [_resident((1, W)), _resident((W, W)), _resident((1, W))],
        out_specs=pl.BlockSpec((None, tc, W), lambda b, i: (b, i, 0)),
        out_shape=jax.ShapeDtypeStruct((B, T, W), BF16),
        scratch_shapes=[pltpu.VMEM((2 * ns, two_half // 2), F32),
                        pltpu.VMEM((tc, two_half), BF16)],
        compiler_params=_cparams("parallel", "arbitrary"),
        name="s5_mixer",
    )(main, *consts, d.reshape(1, W), w_glu, b_glu.reshape(1, W))


def _attn_kernel(q_ref, k_ref, v_ref, o_ref, qs_scr, acc_scr, run_scr, *, tq, heads, dh):
    i = pl.program_id(1)
    row = lax.broadcasted_iota(jnp.int32, (tq, tq), 0)
    col = lax.broadcasted_iota(jnp.int32, (tq, tq), 1)
    suffix = (row > col).astype(BF16)
    causal = col < row
    nt = (((1,), (1,)), ((), ()))
    qs_scr[...] = (q_ref[...].astype(F32) * (dh ** -0.5 * LOG2E)).astype(BF16)

    def logs(h, r0):
        hs = slice(h * dh, (h + 1) * dh)
        z = lax.dot_general(qs_scr[:, hs], k_ref[pl.ds(r0, tq), hs], nt, preferred_element_type=F32)
        sp = jnp.log2(1.0 + jnp.exp2(-jnp.abs(z)))
        lb = jnp.minimum(z, 0.0) - sp
        return lb, lb - z

    r_diag = pl.multiple_of(i * tq, tq)
    run_scr[...] = jnp.full(run_scr.shape, NEG_BIG, F32)
    for h in range(heads):
        hs = slice(h * dh, (h + 1) * dh)
        lb, l1 = logs(h, r_diag)
        l1 = jnp.where(causal, l1, 0.0)
        cum = _dot(l1.astype(BF16), suffix)
        wgt = jnp.where(causal, jnp.exp2(lb + cum), 0.0)
        acc_scr[:, hs] = _dot(wgt.astype(BF16), v_ref[pl.ds(r_diag, tq), hs])
        run_scr[:, h:h + 1] = cum[:, 0:1] + l1[:, 0:1]

    def cond(carry):
        j, top = carry
        return jnp.logical_and(j < i, top > POW2_ZERO_BELOW)

    def body(carry):
        j, _ = carry
        r0 = pl.multiple_of((i - 1 - j) * tq, tq)
        for h in range(heads):
            hs = slice(h * dh, (h + 1) * dh)
            run = run_scr[:, h:h + 1]
            lb, l1 = logs(h, r0)
            cum = _dot(l1.astype(BF16), suffix)
            wgt = jnp.exp2(lb + cum + run)
            acc_scr[:, hs] += _dot(wgt.astype(BF16), v_ref[pl.ds(r0, tq), hs])
            run_scr[:, h:h + 1] = run + cum[:, 0:1] + l1[:, 0:1]
        return j + 1, jnp.max(run_scr[...])

    lax.while_loop(cond, body, (jnp.int32(0), jnp.max(run_scr[...])))
    o_ref[...] = acc_scr[...].astype(BF16)


def _attention(main, tq):
    B, T, W4 = main.shape
    W = W4 // 4
    heads = W // SB_HEAD_DIM
    return pl.pallas_call(
        functools.partial(_attn_kernel, tq=tq, heads=heads, dh=SB_HEAD_DIM),
        grid=(B, T // tq),
        in_specs=[pl.BlockSpec((None, tq, W), lambda b, i: (b, i, 1)),
                  pl.BlockSpec((None, T, W), lambda b, i: (b, 0, 2)),
                  pl.BlockSpec((None, T, W), lambda b, i: (b, 0, 3))],
        out_specs=pl.BlockSpec((None, tq, W), lambda b, i: (b, i, 0)),
        out_shape=jax.ShapeDtypeStruct((B, T, W), BF16),
        scratch_shapes=[pltpu.VMEM((tq, W), BF16),
                        pltpu.VMEM((tq, W), F32),
                        pltpu.VMEM((tq, LANES), F32)],
        compiler_params=_cparams("parallel", "arbitrary"),
        name="sb_attention",
    )(main, main, main)


def _lru_kernel(x_ref, y_ref, cw_ref, cb_ref, wa_ref, ba_ref, wx_ref, bx_ref, lam_ref, o_ref,
                tail_scr, h_scr, a_scr, b_scr, *, tt, nb):
    rows = tt * nb
    halo = (CONV_WIDTH - 1) * nb

    @pl.when(pl.program_id(0) == 0)
    def _():
        tail_scr[...] = jnp.zeros_like(tail_scr)
        h_scr[...] = jnp.zeros_like(h_scr)

    x = x_ref[...].astype(F32)
    xx = jnp.concatenate([tail_scr[...], x], axis=0)
    tail_scr[...] = x[rows - halo:, :]
    xc = cb_ref[...] + cw_ref[0:1, :] * xx[0:rows, :]
    for j in range(1, CONV_WIDTH):
        xc = xc + cw_ref[j:j + 1, :] * xx[j * nb:j * nb + rows, :]
    xcb = xc.astype(BF16)
    r = _sigmoid(_dot(xcb, wa_ref[...]) + ba_ref[...])
    ig = _sigmoid(_dot(xcb, wx_ref[...]) + bx_ref[...])
    lam = lam_ref[...]
    log_sig_lam = jnp.minimum(lam, 0.0) - jnp.log(1.0 + jnp.exp(-jnp.abs(lam)))
    log_a = LRU_C * r * log_sig_lam
    a_scr[...] = jnp.exp(log_a)
    b_scr[...] = jnp.sqrt(1.0 - jnp.exp(2.0 * log_a)) * (ig * xc)

    def step(t, h):
        r0 = pl.multiple_of(t * nb, nb)
        h = a_scr[pl.ds(r0, nb), :] * h + b_scr[pl.ds(r0, nb), :]
        b_scr[pl.ds(r0, nb), :] = h
        return h

    h_scr[...] = lax.fori_loop(0, tt, step, h_scr[...], unroll=8)
    o_ref[...] = (b_scr[...] * _gelu(y_ref[...].astype(F32))).astype(BF16)


def _block_diag(w):
    n, k, _ = w.shape
    eye = jnp.eye(n, dtype=w.dtype)
    return (eye[:, None, :, None] * w[:, :, None, :]).reshape(n * k, n * k)


def _rglru(x_tm, y_tm, conv_w, conv_b, w_a, b_a, w_x, b_x, lam, nb, tt):
    R, W = x_tm.shape
    T = R // nb
    rows = tt * nb
    row = lambda v: v.reshape(1, W)
    return pl.pallas_call(
        functools.partial(_lru_kernel, tt=tt, nb=nb),
        grid=(T // tt,),
        in_specs=[pl.BlockSpec((rows, W), lambda i: (i, 0)),
                  pl.BlockSpec((rows, W), lambda i: (i, 0)),
                  _resident((CONV_WIDTH, W)), _resident((1, W)),
                  _resident((W, W)), _resident((1, W)),
                  _resident((W, W)), _resident((1, W)), _resident((1, W))],
        out_specs=pl.BlockSpec((rows, W), lambda i: (i, 0)),
        out_shape=jax.ShapeDtypeStruct((R, W), BF16),
        scratch_shapes=[pltpu.VMEM(((CONV_WIDTH - 1) * nb, W), F32),
                        pltpu.VMEM((nb, W), F32),
                        pltpu.VMEM((rows, W), F32),
                        pltpu.VMEM((rows, W), F32)],
        compiler_params=_cparams("arbitrary"),
        name="rglru",
    )(x_tm, y_tm, conv_w, row(conv_b), _block_diag(w_a).astype(BF16), row(b_a),
      _block_diag(w_x).astype(BF16), row(b_x), row(lam))


def _merge_kernel(s5_ref, at_ref, lr_ref, gate_ref, x_ref, wbr_ref, gb_ref, wout_ref, o_ref):
    dm = x_ref.shape[-1]
    branches = (s5_ref[...], at_ref[...], lr_ref[...])
    merged = None
    for n in range(N_BRANCH):
        gate = _sigmoid(gate_ref[:, n * dm:(n + 1) * dm].astype(F32) + gb_ref[:, n * dm:(n + 1) * dm])
        term = gate * _dot(branches[n], wbr_ref[n])
        merged = term if merged is None else merged + term
    o_ref[...] = x_ref[...] + _dot(merged.astype(BF16), wout_ref[...])


def _merge(s5, attn, lru_tm, gates, x, w_branch, gate_b, w_out, tm):
    B, T, D = x.shape
    W = MIX_WIDTH
    tok = lambda width: pl.BlockSpec((None, tm, width), lambda b, i: (b, i, 0))
    return pl.pallas_call(
        _merge_kernel,
        grid=(B, T // tm),
        in_specs=[tok(W), tok(W),
                  pl.BlockSpec((tm, W), lambda b, i: (i, b)),
                  tok(N_BRANCH * D), tok(D),
                  _resident((N_BRANCH, W, D)), _resident((1, N_BRANCH * D)), _resident((D, D))],
        out_specs=tok(D),
        out_shape=jax.ShapeDtypeStruct((B, T, D), F32),
        compiler_params=_cparams("parallel", "parallel"),
        name="merge_out",
    )(s5, attn, lru_tm, gates, x, w_branch, gate_b.reshape(1, N_BRANCH * D), w_out)


def _ffn_kernel(x_ref, g_ref, wg_ref, wu_ref, wd_ref, o_ref, *, tf):
    x = x_ref[...]
    h = _rms(x, g_ref[...]).astype(BF16)
    acc = x
    for c0 in range(0, wg_ref.shape[1], tf):
        gate = _dot(h, wg_ref[:, c0:c0 + tf])
        up = _dot(h, wu_ref[:, c0:c0 + tf])
        act = (gate * _sigmoid(gate) * up).astype(BF16)
        acc = acc + _dot(act, wd_ref[c0:c0 + tf, :])
    o_ref[...] = acc


def _ffn(x2, g, w_gate, w_up, w_down, tm, tf):
    N, D = x2.shape
    F = w_gate.shape[1]
    return pl.pallas_call(
        functools.partial(_ffn_kernel, tf=tf),
        grid=(N // tm,),
        in_specs=[pl.BlockSpec((tm, D), lambda i: (i, 0)), _resident((1, D)),
                  _resident((D, F)), _resident((D, F)), _resident((F, D))],
        out_specs=pl.BlockSpec((tm, D), lambda i: (i, 0)),
        out_shape=jax.ShapeDtypeStruct((N, D), F32),
        compiler_params=_cparams("parallel"),
        name="ffn_swiglu",
    )(x2, g.reshape(1, D), w_gate, w_up, w_down)


def _router_kernel(x_ref, g_ref, rw_ref, rb_ref, h_ref, idx_ref, wgt_ref, *, n_exp):
    h = _rms(x_ref[...], g_ref[...])
    h_ref[...] = h.astype(BF16)
    logits = jnp.dot(h, rw_ref[...], preferred_element_type=F32, precision=lax.Precision.HIGHEST)
    lane = lax.broadcasted_iota(jnp.int32, logits.shape, 1)
    lg = jnp.where(lane < n_exp, logits + rb_ref[...], NEG_BIG)
    m1 = jnp.max(lg, axis=1, keepdims=True)
    i1 = jnp.min(jnp.where(lg == m1, lane, LANES), axis=1, keepdims=True)
    lg2 = jnp.where(lane == i1, NEG_BIG, lg)
    m2 = jnp.max(lg2, axis=1, keepdims=True)
    i2 = jnp.min(jnp.where(lg2 == m2, lane, LANES), axis=1, keepdims=True)
    e = jnp.exp(m2 - m1)
    w1 = 1.0 / (1.0 + e)
    w2 = e / (1.0 + e)
    idx_ref[...] = jnp.where(lane == 0, i1, jnp.where(lane == 1, i2, 0))
    wgt_ref[...] = jnp.where(lane == 0, w1, jnp.where(lane == 1, w2, 0.0))


def _router(x2, g, router_w, router_b, tm):
    N, D = x2.shape
    E = router_w.shape[1]
    rw = jnp.zeros((D, LANES), F32).at[:, :E].set(router_w)
    rb = jnp.zeros((1, LANES), F32).at[0, :E].set(router_b)
    return pl.pallas_call(
        functools.partial(_router_kernel, n_exp=E),
        grid=(N // tm,),
        in_specs=[pl.BlockSpec((tm, D), lambda i: (i, 0)), _resident((1, D)),
                  _resident((D, LANES)), _resident((1, LANES))],
        out_specs=[pl.BlockSpec((tm, D), lambda i: (i, 0)),
                   pl.BlockSpec((tm, LANES), lambda i: (i, 0)),
                   pl.BlockSpec((tm, LANES), lambda i: (i, 0))],
        out_shape=[jax.ShapeDtypeStruct((N, D), BF16),
                   jax.ShapeDtypeStruct((N, LANES), jnp.int32),
                   jax.ShapeDtypeStruct((N, LANES), F32)],
        compiler_params=_cparams("parallel"),
        name="moe_router",
    )(x2, g.reshape(1, D), rw, rb)


def _expert_kernel(te_ref, nu_ref, xs_ref, wg_ref, wu_ref, wd_ref, o_ref, acc_ref):
    t = pl.program_id(0)
    f = pl.program_id(1)
    used = t < nu_ref[0]

    @pl.when(f == 0)
    def _():
        acc_ref[...] = jnp.zeros_like(acc_ref)

    @pl.when(used)
    def _():
        xs = xs_ref[...]
        gate = _dot(xs, wg_ref[...])
        up = _dot(xs, wu_ref[...])
        act = (gate * _sigmoid(gate) * up).astype(BF16)
        acc_ref[...] += _dot(act, wd_ref[...])

    @pl.when(f == pl.num_programs(1) - 1)
    def _():
        o_ref[...] = acc_ref[...].astype(BF16)


def _experts(xs, tile_expert, n_used, w_gate, w_up, w_down, tm, tf):
    S, D = xs.shape
    F = w_gate.shape[2]
    return pl.pallas_call(
        _expert_kernel,
        grid_spec=pltpu.PrefetchScalarGridSpec(
            num_scalar_prefetch=2,
            grid=(S // tm, F // tf),
            in_specs=[pl.BlockSpec((tm, D), lambda t, f, te, nu: (t, 0)),
                      pl.BlockSpec((None, D, tf), lambda t, f, te, nu: (te[t], 0, f)),
                      pl.BlockSpec((None, D, tf), lambda t, f, te, nu: (te[t], 0, f)),
                      pl.BlockSpec((None, tf, D), lambda t, f, te, nu: (te[t], f, 0))],
            out_specs=pl.BlockSpec((tm, D), lambda t, f, te, nu: (t, 0)),
            scratch_shapes=[pltpu.VMEM((tm, D), F32)]),
        out_shape=jax.ShapeDtypeStruct((S, D), BF16),
        compiler_params=_cparams("parallel", "arbitrary"),
        name="moe_experts",
    )(tile_expert, n_used, xs, w_gate, w_up, w_down)


def _route(idx, n_exp, tm):
    N, K = idx.shape
    flat = idx.reshape(-1)
    onehot = (flat[:, None] == jnp.arange(n_exp)[None, :]).astype(jnp.int32)
    rank = jnp.cumsum(onehot, axis=0) - onehot
    counts = jnp.sum(onehot, axis=0)
    padded = ((counts + tm - 1) // tm) * tm
    ends = jnp.cumsum(padded)
    starts = ends - padded
    slot = jnp.sum((starts[None, :] + rank) * onehot, axis=1)
    S = N * K + n_exp * tm
    src = jnp.zeros((S,), jnp.int32).at[slot].set(jnp.arange(N * K, dtype=jnp.int32) // K)
    tile_start = jnp.arange(S // tm, dtype=jnp.int32) * tm
    last_start = jnp.maximum(ends[-1] - tm, 0)
    tile_expert = jnp.sum((jnp.minimum(tile_start, last_start)[:, None] >= ends[None, :])
                          .astype(jnp.int32), axis=1)
    tile_expert = jnp.minimum(tile_expert, n_exp - 1)
    n_used = (ends[-1] // tm).astype(jnp.int32).reshape(1)
    return src, slot.reshape(N, K), tile_expert, n_used


def _combine_kernel(x_ref, y0_ref, y1_ref, wgt_ref, g_ref, o_ref, *, final_norm):
    w0 = wgt_ref[:, 0:1]
    w1 = wgt_ref[:, 1:2]
    x = x_ref[...] + w0 * y0_ref[...].astype(F32) + w1 * y1_ref[...].astype(F32)
    o_ref[...] = _rms(x, g_ref[...]) if final_norm else x


def _combine(x2, y0, y1, wgt, g, tm, final_norm):
    N, D = x2.shape
    return pl.pallas_call(
        functools.partial(_combine_kernel, final_norm=final_norm),
        grid=(N // tm,),
        in_specs=[pl.BlockSpec((tm, D), lambda i: (i, 0)),
                  pl.BlockSpec((tm, D), lambda i: (i, 0)),
                  pl.BlockSpec((tm, D), lambda i: (i, 0)),
                  pl.BlockSpec((tm, LANES), lambda i: (i, 0)),
                  _resident((1, D))],
        out_specs=pl.BlockSpec((tm, D), lambda i: (i, 0)),
        out_shape=jax.ShapeDtypeStruct((N, D), F32),
        compiler_params=_cparams("parallel"),
        name="moe_combine",
    )(x2, y0, y1, wgt, g.reshape(1, D))


def _moe(x2, g, router_w, router_b, w_gate, w_up, w_down, final_g, tm_tok, tm_exp, tf):
    n_exp = router_w.shape[1]
    h, idx, wgt = _router(x2, g, router_w, router_b, tm_tok)
    src, slot, tile_expert, n_used = _route(idx[:, :TOP_K], n_exp, tm_exp)
    xs = jnp.take(h, src, axis=0)
    ys = _experts(xs, tile_expert, n_used, w_gate, w_up, w_down, tm_exp, tf)
    y0 = jnp.take(ys, slot[:, 0], axis=0)
    y1 = jnp.take(ys, slot[:, 1], axis=0)
    gn = final_g if final_g is not None else jnp.ones((x2.shape[1],), F32)
    return _combine(x2, y0, y1, wgt, gn, tm_tok, final_g is not None)


def _final_norm_kernel(x_ref, g_ref, o_ref):
    o_ref[...] = _rms(x_ref[...], g_ref[...])


def _final_norm(x2, g, tm):
    N, D = x2.shape
    return pl.pallas_call(
        _final_norm_kernel,
        grid=(N // tm,),
        in_specs=[pl.BlockSpec((tm, D), lambda i: (i, 0)), _resident((1, D))],
        out_specs=pl.BlockSpec((tm, D), lambda i: (i, 0)),
        out_shape=jax.ShapeDtypeStruct((N, D), F32),
        compiler_params=_cparams("parallel"),
        name="final_norm",
    )(x2, g.reshape(1, D))


def _tile(n, pref):
    t = min(n, pref)
    while n % t:
        t //= 2
    return t


def kernel(x, mix_norm_g, w_in, gate_b, s5_lambda_re, s5_lambda_im, s5_log_dt, s5_b_re, s5_b_im, s5_c_re, s5_c_im, s5_d, s5_w_glu, s5_b_glu, conv_w, conv_b, lru_w_a, lru_b_a, lru_w_x, lru_b_x, lru_lambda, w_branch, w_out, ffn_norm_g, ffn_w_gate, ffn_w_up, ffn_w_down, router_w, router_b, moe_w_gate, moe_w_up, moe_w_down, final_norm_g):
    B, T, D = x.shape
    depth = w_in.shape[0]
    W = MIX_WIDTH
    N = B * T
    tm = _tile(T, 512)
    tq = _tile(T, 256)
    tt = _tile(T, 128)
    L = _tile(T, S5_CHUNK)
    tm_exp = _tile(N, 512)
    x = x.astype(F32)
    for layer in range(depth):
        main, xl_tm, yl_tm, gates = _norm_proj(x, mix_norm_g[layer], w_in[layer].astype(BF16), tm)
        consts = _s5_consts(s5_lambda_re[layer], s5_lambda_im[layer], s5_log_dt[layer],
                            s5_b_re[layer], s5_b_im[layer], s5_c_re[layer], s5_c_im[layer], L)
        s5 = _s5(main, consts, s5_d[layer], s5_w_glu[layer].astype(BF16), s5_b_glu[layer], tm, L)
        attn = _attention(main, tq)
        lru = _rglru(xl_tm.reshape(T * B, W), yl_tm.reshape(T * B, W), conv_w[layer], conv_b[layer],
                     lru_w_a[layer], lru_b_a[layer], lru_w_x[layer], lru_b_x[layer],
                     lru_lambda[layer], B, tt)
        x = _merge(s5, attn, lru.reshape(T, B * W), gates, x, w_branch[layer].astype(BF16),
                   gate_b[layer], w_out[layer].astype(BF16), tm)
        x2 = x.reshape(N, D)
        last = layer == depth - 1
        j = layer // 2
        if layer % 2 == 0:
            x2 = _ffn(x2, ffn_norm_g[layer], ffn_w_gate[j].astype(BF16), ffn_w_up[j].astype(BF16),
                      ffn_w_down[j].astype(BF16), tm, ffn_w_gate.shape[2] // 2)
            if last:
                x2 = _final_norm(x2, final_norm_g, tm)
        else:
            x2 = _moe(x2, ffn_norm_g[layer], router_w[j], router_b[j], moe_w_gate[j].astype(BF16),
                      moe_w_up[j].astype(BF16), moe_w_down[j].astype(BF16),
                      final_norm_g if last else None, tm, tm_exp, 512)
        x = x2.reshape(B, T, D)
    return x
```

```python
import functools
import math

import jax
import jax.numpy as jnp
from jax import lax
from jax.experimental import pallas as pl
from jax.experimental.pallas import tpu as pltpu

F32 = jnp.float32
BF16 = jnp.bfloat16

RMS_EPS = 1e-6
MIX_WIDTH = 384
S5_GROUP = 16
S5_STATE = 64
S5_CHUNK = 128
SB_HEAD_DIM = 64
LRU_BLOCKS = 6
CONV_WIDTH = 4
LRU_C = 8.0
N_BRANCH = 3
TOP_K = 2
LANES = 128
NEG_BIG = -1e30
LOG2E = 1.4426950408889634
POW2_ZERO_BELOW = -150.0
VMEM_LIMIT = 56 * 1024 * 1024


def _cparams(*sem):
    return pltpu.CompilerParams(dimension_semantics=sem, vmem_limit_bytes=VMEM_LIMIT)


def _resident(shape):
    n = len(shape)
    return pl.BlockSpec(shape, lambda *_: (0,) * n, pipeline_mode=pl.Buffered(1))


def _rms(xf, g):
    return xf * lax.rsqrt(jnp.mean(xf * xf, axis=-1, keepdims=True) + RMS_EPS) * g


def _gelu(x):
    c = math.sqrt(2.0 / math.pi)
    return 0.5 * x * (1.0 + jnp.tanh(c * (x + 0.044715 * (x * x * x))))


def _sigmoid(x):
    return 1.0 / (1.0 + jnp.exp(-x))


def _dot(a, b):
    return jnp.dot(a, b, preferred_element_type=F32)


def _norm_proj_kernel(x_ref, g_ref, w_ref, main_ref, xl_ref, yl_ref, gate_ref, *, w):
    h = _rms(x_ref[...], g_ref[...]).astype(BF16)
    main_ref[...] = _dot(h, w_ref[:, 0:4 * w]).astype(BF16)
    xl_ref[...] = _dot(h, w_ref[:, 4 * w:5 * w]).astype(BF16)
    yl_ref[...] = _dot(h, w_ref[:, 5 * w:6 * w]).astype(BF16)
    d = gate_ref.shape[-1] // N_BRANCH
    for n in range(N_BRANCH):
        c0 = 6 * w + n * d
        gate_ref[:, n * d:(n + 1) * d] = _dot(h, w_ref[:, c0:c0 + d]).astype(BF16)


def _norm_proj(x, g, w_in, layer, tm):
    B, T, D = x.shape
    W = MIX_WIDTH
    return pl.pallas_call(
        functools.partial(_norm_proj_kernel, w=W),
        grid=(B, T // tm),
        in_specs=[pl.BlockSpec((None, tm, D), lambda b, i: (b, i, 0)),
                  _resident((1, D)),
                  pl.BlockSpec((None,) + w_in.shape[1:], lambda b, i: (layer, 0, 0),
                               pipeline_mode=pl.Buffered(1))],
        out_specs=[pl.BlockSpec((None, tm, 4 * W), lambda b, i: (b, i, 0)),
                   pl.BlockSpec((tm, W), lambda b, i: (i, b)),
                   pl.BlockSpec((tm, W), lambda b, i: (i, b)),
                   pl.BlockSpec((None, tm, N_BRANCH * D), lambda b, i: (b, i, 0))],
        out_shape=[jax.ShapeDtypeStruct((B, T, 4 * W), BF16),
                   jax.ShapeDtypeStruct((T, B * W), BF16),
                   jax.ShapeDtypeStruct((T, B * W), BF16),
                   jax.ShapeDtypeStruct((B, T, N_BRANCH * D), BF16)],
        compiler_params=_cparams("parallel", "parallel"),
        name="norm_proj",
    )(x, g.reshape(1, D), w_in)


def _s5_consts(lam_re, lam_im, log_dt, b_re, b_im, c_re, c_im, L):
    G, P, H = b_re.shape
    ns = (G * H) // LANES
    gs = G // ns
    dt = jnp.exp(log_dt)[:, None]
    ea = jnp.exp(lam_re * dt)
    abar_re = ea * jnp.cos(lam_im * dt)
    abar_im = ea * jnp.sin(lam_im * dt)
    den = lam_re * lam_re + lam_im * lam_im
    nr = abar_re - 1.0
    coef_re = (nr * lam_re + abar_im * lam_im) / den
    coef_im = (abar_im * lam_re - nr * lam_im) / den
    bb_re = coef_re[..., None] * b_re - coef_im[..., None] * b_im
    bb_im = coef_re[..., None] * b_im + coef_im[..., None] * b_re
    eye = jnp.eye(gs, dtype=F32)

    def in_slab(bb):
        return jnp.einsum('sgph,gk->sghkp', bb.reshape(ns, gs, P, H), eye).reshape(ns, gs * H, gs * P)

    def out_slab(c):
        return jnp.einsum('sgop,gk->sgpko', c.reshape(ns, gs, H, P), eye).reshape(ns, gs * P, gs * H)

    b_slab = jnp.concatenate([in_slab(bb_re), in_slab(bb_im)], axis=-1).astype(BF16)
    c_slab = jnp.concatenate([out_slab(c_re), -out_slab(c_im)], axis=1).astype(BF16)

    def power(k):
        mag = jnp.exp((lam_re * dt)[None] * k[:, None, None])
        ang = (lam_im * dt)[None] * k[:, None, None]
        lay = lambda v: v.reshape(k.shape[0], ns, gs * P).transpose(1, 0, 2)
        return lay(mag * jnp.cos(ang)), lay(mag * jnp.sin(ang))

    t = jnp.arange(L, dtype=F32)
    m = float(L // 2)
    em_re, em_im = power(m - t)
    ep_re, ep_im = power(t - m)
    vf_re, vf_im = power(jnp.full((1,), m + 1.0, F32))
    return b_slab, c_slab, em_re, em_im, ep_re, ep_im, vf_re, vf_im


def _s5_kernel(u_ref, b_ref, c_ref, emr_ref, emi_ref, epr_ref, epi_ref, vfr_ref, vfi_ref,
               d_ref, wglu_ref, bglu_ref, o_ref, carry_scr, s_scr, *, L):
    tc = u_ref.shape[0]
    ns = b_ref.shape[0]
    half = b_ref.shape[2] // 2

    @pl.when(pl.program_id(1) == 0)
    def _():
        carry_scr[...] = jnp.zeros_like(carry_scr)

    row = lax.broadcasted_iota(jnp.int32, (L, L), 0)
    col = lax.broadcasted_iota(jnp.int32, (L, L), 1)
    tri = (col <= row).astype(BF16)
    ys = []
    for j in range(ns):
        u = u_ref[:, j * LANES:(j + 1) * LANES]
        bu = _dot(u, b_ref[j])
        emr, emi, epr, epi = emr_ref[j], emi_ref[j], epr_ref[j], epi_ref[j]
        vfr, vfi = vfr_ref[j], vfi_ref[j]
        c_re = carry_scr[2 * j:2 * j + 1, :]
        c_im = carry_scr[2 * j + 1:2 * j + 2, :]
        for c in range(tc // L):
            bur = bu[c * L:(c + 1) * L, :half]
            bui = bu[c * L:(c + 1) * L, half:]
            scaled = jnp.concatenate([bur * emr - bui * emi, bur * emi + bui * emr], axis=1)
            pre = _dot(tri, scaled.astype(BF16))
            pr = pre[:, :half] + (vfr * c_re - vfi * c_im)
            pi = pre[:, half:] + (vfr * c_im + vfi * c_re)
            s_re = pr * epr - pi * epi
            s_im = pr * epi + pi * epr
            c_re = s_re[L - 1:L, :]
            c_im = s_im[L - 1:L, :]
            s_scr[c * L:(c + 1) * L, :] = jnp.concatenate([s_re, s_im], axis=1).astype(BF16)
        carry_scr[2 * j:2 * j + 1, :] = c_re
        carry_scr[2 * j + 1:2 * j + 2, :] = c_im
        ys.append(_dot(s_scr[...], c_ref[j]))
    y = jnp.concatenate(ys, axis=1) + d_ref[...] * u_ref[...].astype(F32)
    ya = _gelu(y)
    o_ref[...] = (ya * _sigmoid(_dot(ya.astype(BF16), wglu_ref[...]) + bglu_ref[...])).astype(BF16)


def _s5(main, consts, d, w_glu, b_glu, tc, L):
    B, T, _ = main.shape
    W = MIX_WIDTH
    b_slab, c_slab = consts[0], consts[1]
    ns, _, two_half = b_slab.shape
    return pl.pallas_call(
        functools.partial(_s5_kernel, L=L),
        grid=(B, T // tc),
        in_specs=[pl.BlockSpec((None, tc, W), lambda b, i: (b, i, 0))]
        + [_resident(c.shape) for c in consts]
        + [_resident((1, W)), _resident((W, W)), _resident((1, W))],
        out_specs=pl.BlockSpec((None, tc, W), lambda b, i: (b, i, 0)),
        out_shape=jax.ShapeDtypeStruct((B, T, W), BF16),
        scratch_shapes=[pltpu.VMEM((2 * ns, two_half // 2), F32),
                        pltpu.VMEM((tc, two_half), BF16)],
        compiler_params=_cparams("parallel", "arbitrary"),
        name="s5_mixer",
    )(main, *consts, d.reshape(1, W), w_glu, b_glu.reshape(1, W))


def _attn_kernel(q_ref, k_ref, v_ref, o_ref, qs_scr, acc_scr, run_scr, *, tq, heads, dh):
    i = pl.program_id(1)
    row = lax.broadcasted_iota(jnp.int32, (tq, tq), 0)
    col = lax.broadcasted_iota(jnp.int32, (tq, tq), 1)
    suffix = (row > col).astype(BF16)
    causal = col < row
    nt = (((1,), (1,)), ((), ()))
    qs_scr[...] = (q_ref[...].astype(F32) * (dh ** -0.5 * LOG2E)).astype(BF16)

    def logs(h, r0):
        hs = slice(h * dh, (h + 1) * dh)
        z = lax.dot_general(qs_scr[:, hs], k_ref[pl.ds(r0, tq), hs], nt, preferred_element_type=F32)
        sp = jnp.log2(1.0 + jnp.exp2(-jnp.abs(z)))
        lb = jnp.minimum(z, 0.0) - sp
        return lb, lb - z

    r_diag = pl.multiple_of(i * tq, tq)
    run_scr[...] = jnp.full(run_scr.shape, NEG_BIG, F32)
    for h in range(heads):
        hs = slice(h * dh, (h + 1) * dh)
        lb, l1 = logs(h, r_diag)
        l1 = jnp.where(causal, l1, 0.0)
        cum = _dot(l1.astype(BF16), suffix)
        wgt = jnp.where(causal, jnp.exp2(lb + cum), 0.0)
        acc_scr[:, hs] = _dot(wgt.astype(BF16), v_ref[pl.ds(r_diag, tq), hs])
        run_scr[:, h:h + 1] = cum[:, 0:1] + l1[:, 0:1]

    def cond(carry):
        j, top = carry
        return jnp.logical_and(j < i, top > POW2_ZERO_BELOW)

    def body(carry):
        j, _ = carry
        r0 = pl.multiple_of((i - 1 - j) * tq, tq)
        for h in range(heads):
            hs = slice(h * dh, (h + 1) * dh)
            run = run_scr[:, h:h + 1]
            lb, l1 = logs(h, r0)
            cum = _dot(l1.astype(BF16), suffix)
            wgt = jnp.exp2(lb + cum + run)
            acc_scr[:, hs] += _dot(wgt.astype(BF16), v_ref[pl.ds(r0, tq), hs])
            run_scr[:, h:h + 1] = run + cum[:, 0:1] + l1[:, 0:1]
        return j + 1, jnp.max(run_scr[...])

    lax.while_loop(cond, body, (jnp.int32(0), jnp.max(run_scr[...])))
    o_ref[...] = acc_scr[...].astype(BF16)


def _attention(main, tq):
    B, T, W4 = main.shape
    W = W4 // 4
    heads = W // SB_HEAD_DIM
    return pl.pallas_call(
        functools.partial(_attn_kernel, tq=tq, heads=heads, dh=SB_HEAD_DIM),
        grid=(B, T // tq),
        in_specs=[pl.BlockSpec((None, tq, W), lambda b, i: (b, i, 1)),
                  pl.BlockSpec((None, T, W), lambda b, i: (b, 0, 2)),
                  pl.BlockSpec((None, T, W), lambda b, i: (b, 0, 3))],
        out_specs=pl.BlockSpec((None, tq, W), lambda b, i: (b, i, 0)),
        out_shape=jax.ShapeDtypeStruct((B, T, W), BF16),
        scratch_shapes=[pltpu.VMEM((tq, W), BF16),
                        pltpu.VMEM((tq, W), F32),
                        pltpu.VMEM((tq, LANES), F32)],
        compiler_params=_cparams("parallel", "arbitrary"),
        name="sb_attention",
    )(main, main, main)


def _lru_kernel(x_ref, y_ref, cw_ref, cb_ref, wa_ref, ba_ref, wx_ref, bx_ref, lam_ref, o_ref,
                tail_scr, h_scr, a_scr, b_scr, *, tt, nb):
    rows = tt * nb
    halo = (CONV_WIDTH - 1) * nb

    @pl.when(pl.program_id(0) == 0)
    def _():
        tail_scr[...] = jnp.zeros_like(tail_scr)
        h_scr[...] = jnp.zeros_like(h_scr)

    x = x_ref[...].astype(F32)
    xx = jnp.concatenate([tail_scr[...], x], axis=0)
    tail_scr[...] = x[rows - halo:, :]
    xc = cb_ref[...] + cw_ref[0:1, :] * xx[0:rows, :]
    for j in range(1, CONV_WIDTH):
        xc = xc + cw_ref[j:j + 1, :] * xx[j * nb:j * nb + rows, :]
    xcb = xc.astype(BF16)
    r = _sigmoid(_dot(xcb, wa_ref[...]) + ba_ref[...])
    ig = _sigmoid(_dot(xcb, wx_ref[...]) + bx_ref[...])
    lam = lam_ref[...]
    log_sig_lam = jnp.minimum(lam, 0.0) - jnp.log(1.0 + jnp.exp(-jnp.abs(lam)))
    log_a = LRU_C * r * log_sig_lam
    a_scr[...] = jnp.exp(log_a)
    b_scr[...] = jnp.sqrt(1.0 - jnp.exp(2.0 * log_a)) * (ig * xc)

    def step(t, h):
        r0 = pl.multiple_of(t * nb, nb)
        h = a_scr[pl.ds(r0, nb), :] * h + b_scr[pl.ds(r0, nb), :]
        b_scr[pl.ds(r0, nb), :] = h
        return h

    h_scr[...] = lax.fori_loop(0, tt, step, h_scr[...], unroll=8)
    o_ref[...] = (b_scr[...] * _gelu(y_ref[...].astype(F32))).astype(BF16)


def _block_diag(w):
    n, k, _ = w.shape
    eye = jnp.eye(n, dtype=w.dtype)
    return (eye[:, None, :, None] * w[:, :, None, :]).reshape(n * k, n * k)


def _rglru(x_tm, y_tm, conv_w, conv_b, w_a, b_a, w_x, b_x, lam, nb, tt):
    R, W = x_tm.shape
    T = R // nb
    rows = tt * nb
    row = lambda v: v.reshape(1, W)
    return pl.pallas_call(
        functools.partial(_lru_kernel, tt=tt, nb=nb),
        grid=(T // tt,),
        in_specs=[pl.BlockSpec((rows, W), lambda i: (i, 0)),
                  pl.BlockSpec((rows, W), lambda i: (i, 0)),
                  _resident((CONV_WIDTH, W)), _resident((1, W)),
                  _resident((W, W)), _resident((1, W)),
                  _resident((W, W)), _resident((1, W)), _resident((1, W))],
        out_specs=pl.BlockSpec((rows, W), lambda i: (i, 0)),
        out_shape=jax.ShapeDtypeStruct((R, W), BF16),
        scratch_shapes=[pltpu.VMEM(((CONV_WIDTH - 1) * nb, W), F32),
                        pltpu.VMEM((nb, W), F32),
                        pltpu.VMEM((rows, W), F32),
                        pltpu.VMEM((rows, W), F32)],
        compiler_params=_cparams("arbitrary"),
        name="rglru",
    )(x_tm, y_tm, conv_w, row(conv_b), _block_diag(w_a).astype(BF16), row(b_a),
      _block_diag(w_x).astype(BF16), row(b_x), row(lam))


def _merge_kernel(s5_ref, at_ref, lr_ref, gate_ref, x_ref, wbr_ref, gb_ref, wout_ref, o_ref):
    dm = x_ref.shape[-1]
    branches = (s5_ref[...], at_ref[...], lr_ref[...])
    merged = None
    for n in range(N_BRANCH):
        gate = _sigmoid(gate_ref[:, n * dm:(n + 1) * dm].astype(F32) + gb_ref[:, n * dm:(n + 1) * dm])
        term = gate * _dot(branches[n], wbr_ref[n])
        merged = term if merged is None else merged + term
    o_ref[...] = x_ref[...] + _dot(merged.astype(BF16), wout_ref[...])


def _merge(s5, attn, lru_tm, gates, x, w_branch, gate_b, w_out, tm):
    B, T, D = x.shape
    W = MIX_WIDTH
    tok = lambda width: pl.BlockSpec((None, tm, width), lambda b, i: (b, i, 0))
    return pl.pallas_call(
        _merge_kernel,
        grid=(B, T // tm),
        in_specs=[tok(W), tok(W),
                  pl.BlockSpec((tm, W), lambda b, i: (i, b)),
                  tok(N_BRANCH * D), tok(D),
                  _resident((N_BRANCH, W, D)), _resident((1, N_BRANCH * D)), _resident((D, D))],
        out_specs=tok(D),
        out_shape=jax.ShapeDtypeStruct((B, T, D), F32),
        compiler_params=_cparams("parallel", "parallel"),
        name="merge_out",
    )(s5, attn, lru_tm, gates, x, w_branch, gate_b.reshape(1, N_BRANCH * D), w_out)


def _ffn_kernel(x_ref, g_ref, wg_ref, wu_ref, wd_ref, o_ref, *, tf):
    x = x_ref[...]
    h = _rms(x, g_ref[...]).astype(BF16)
    acc = x
    for c0 in range(0, wg_ref.shape[1], tf):
        gate = _dot(h, wg_ref[:, c0:c0 + tf])
        up = _dot(h, wu_ref[:, c0:c0 + tf])
        act = (gate * _sigmoid(gate) * up).astype(BF16)
        acc = acc + _dot(act, wd_ref[c0:c0 + tf, :])
    o_ref[...] = acc


def _ffn(x2, g, w_gate, w_up, w_down, tm, tf):
    N, D = x2.shape
    F = w_gate.shape[1]
    return pl.pallas_call(
        functools.partial(_ffn_kernel, tf=tf),
        grid=(N // tm,),
        in_specs=[pl.BlockSpec((tm, D), lambda i: (i, 0)), _resident((1, D)),
                  _resident((D, F)), _resident((D, F)), _resident((F, D))],
        out_specs=pl.BlockSpec((tm, D), lambda i: (i, 0)),
        out_shape=jax.ShapeDtypeStruct((N, D), F32),
        compiler_params=_cparams("parallel"),
        name="ffn_swiglu",
    )(x2, g.reshape(1, D), w_gate, w_up, w_down)


def _router_kernel(x_ref, g_ref, rw_ref, rb_ref, h_ref, idx_ref, wgt_ref, *, n_exp):
    h = _rms(x_ref[...], g_ref[...])
    h_ref[...] = h.astype(BF16)
    logits = jnp.dot(h, rw_ref[...], preferred_element_type=F32, precision=lax.Precision.HIGHEST)
    lane = lax.broadcasted_iota(jnp.int32, logits.shape, 1)
    lg = jnp.where(lane < n_exp, logits + rb_ref[...], NEG_BIG)
    m1 = jnp.max(lg, axis=1, keepdims=True)
    i1 = jnp.min(jnp.where(lg == m1, lane, LANES), axis=1, keepdims=True)
    lg2 = jnp.where(lane == i1, NEG_BIG, lg)
    m2 = jnp.max(lg2, axis=1, keepdims=True)
    i2 = jnp.min(jnp.where(lg2 == m2, lane, LANES), axis=1, keepdims=True)
    e = jnp.exp(m2 - m1)
    w1 = 1.0 / (1.0 + e)
    w2 = e / (1.0 + e)
    idx_ref[...] = jnp.where(lane == 0, i1, jnp.where(lane == 1, i2, 0))
    wgt_ref[...] = jnp.where(lane == 0, w1, jnp.where(lane == 1, w2, 0.0))


def _router(x2, g, router_w, router_b, tm):
    N, D = x2.shape
    E = router_w.shape[1]
    rw = jnp.zeros((D, LANES), F32).at[:, :E].set(router_w)
    rb = jnp.zeros((1, LANES), F32).at[0, :E].set(router_b)
    return pl.pallas_call(
        functools.partial(_router_kernel, n_exp=E),
        grid=(N // tm,),
        in_specs=[pl.BlockSpec((tm, D), lambda i: (i, 0)), _resident((1, D)),
                  _resident((D, LANES)), _resident((1, LANES))],
        out_specs=[pl.BlockSpec((tm, D), lambda i: (i, 0)),
                   pl.BlockSpec((tm, LANES), lambda i: (i, 0)),
                   pl.BlockSpec((tm, LANES), lambda i: (i, 0))],
        out_shape=[jax.ShapeDtypeStruct((N, D), BF16),
                   jax.ShapeDtypeStruct((N, LANES), jnp.int32),
                   jax.ShapeDtypeStruct((N, LANES), F32)],
        compiler_params=_cparams("parallel"),
        name="moe_router",
    )(x2, g.reshape(1, D), rw, rb)


def _expert_kernel(te_ref, nu_ref, xs_ref, wg_ref, wu_ref, wd_ref, o_ref, acc_ref):
    t = pl.program_id(0)
    f = pl.program_id(1)
    used = t < nu_ref[0]

    @pl.when(f == 0)
    def _():
        acc_ref[...] = jnp.zeros_like(acc_ref)

    @pl.when(used)
    def _():
        xs = xs_ref[...]
        gate = _dot(xs, wg_ref[...].astype(BF16))
        up = _dot(xs, wu_ref[...].astype(BF16))
        act = (gate * _sigmoid(gate) * up).astype(BF16)
        acc_ref[...] += _dot(act, wd_ref[...].astype(BF16))

    @pl.when(f == pl.num_programs(1) - 1)
    def _():
        o_ref[...] = acc_ref[...].astype(BF16)


def _experts(xs, tile_expert, n_used, w_gate, w_up, w_down, j, tm, tf):
    S, D = xs.shape
    F = w_gate.shape[3]
    return pl.pallas_call(
        _expert_kernel,
        grid_spec=pltpu.PrefetchScalarGridSpec(
            num_scalar_prefetch=2,
            grid=(S // tm, F // tf),
            in_specs=[pl.BlockSpec((tm, D), lambda t, f, te, nu: (t, 0)),
                      pl.BlockSpec((None, None, D, tf), lambda t, f, te, nu: (j, te[t], 0, f)),
                      pl.BlockSpec((None, None, D, tf), lambda t, f, te, nu: (j, te[t], 0, f)),
                      pl.BlockSpec((None, None, tf, D), lambda t, f, te, nu: (j, te[t], f, 0))],
            out_specs=pl.BlockSpec((tm, D), lambda t, f, te, nu: (t, 0)),
            scratch_shapes=[pltpu.VMEM((tm, D), F32)]),
        out_shape=jax.ShapeDtypeStruct((S, D), BF16),
        compiler_params=_cparams("parallel", "arbitrary"),
        name="moe_experts",
    )(tile_expert, n_used, xs, w_gate, w_up, w_down)


def _route(idx, n_exp, tm):
    N, K = idx.shape
    flat = idx.reshape(-1)
    onehot = (flat[:, None] == jnp.arange(n_exp)[None, :]).astype(jnp.int32)
    rank = jnp.cumsum(onehot, axis=0) - onehot
    counts = jnp.sum(onehot, axis=0)
    padded = ((counts + tm - 1) // tm) * tm
    ends = jnp.cumsum(padded)
    starts = ends - padded
    slot = jnp.sum((starts[None, :] + rank) * onehot, axis=1)
    S = N * K + n_exp * tm
    tile_start = jnp.arange(S // tm, dtype=jnp.int32) * tm
    last_start = jnp.maximum(ends[-1] - tm, 0)
    tile_expert = jnp.sum((jnp.minimum(tile_start, last_start)[:, None] >= ends[None, :])
                          .astype(jnp.int32), axis=1)
    tile_expert = jnp.minimum(tile_expert, n_exp - 1)
    n_used = (ends[-1] // tm).astype(jnp.int32).reshape(1)
    order = jnp.argsort(flat, stable=True).astype(jnp.int32)
    per_slot = lambda v: jnp.repeat(v[tile_expert], tm)
    within = jnp.arange(S, dtype=jnp.int32) - per_slot(starts)
    compact = jnp.clip(per_slot(jnp.cumsum(counts) - counts) + within, 0, N * K - 1)
    src = jnp.where(within < per_slot(counts), order[compact] // K, 0)
    return src, slot.reshape(N, K), tile_expert, n_used


def _combine_kernel(x_ref, y0_ref, y1_ref, wgt_ref, g_ref, o_ref, *, final_norm):
    w0 = wgt_ref[:, 0:1]
    w1 = wgt_ref[:, 1:2]
    x = x_ref[...] + w0 * y0_ref[...].astype(F32) + w1 * y1_ref[...].astype(F32)
    o_ref[...] = _rms(x, g_ref[...]) if final_norm else x


def _combine(x2, y0, y1, wgt, g, tm, final_norm):
    N, D = x2.shape
    return pl.pallas_call(
        functools.partial(_combine_kernel, final_norm=final_norm),
        grid=(N // tm,),
        in_specs=[pl.BlockSpec((tm, D), lambda i: (i, 0)),
                  pl.BlockSpec((tm, D), lambda i: (i, 0)),
                  pl.BlockSpec((tm, D), lambda i: (i, 0)),
                  pl.BlockSpec((tm, LANES), lambda i: (i, 0)),
                  _resident((1, D))],
        out_specs=pl.BlockSpec((tm, D), lambda i: (i, 0)),
        out_shape=jax.ShapeDtypeStruct((N, D), F32),
        compiler_params=_cparams("parallel"),
        name="moe_combine",
    )(x2, y0, y1, wgt, g.reshape(1, D))


def _moe(x2, g, router_w, router_b, w_gate, w_up, w_down, j, final_g, tm_tok, tm_exp, tf):
    n_exp = router_w.shape[1]
    h, idx, wgt = _router(x2, g, router_w, router_b, tm_tok)
    src, slot, tile_expert, n_used = _route(idx[:, :TOP_K], n_exp, tm_exp)
    xs = jnp.take(h, src, axis=0, mode="clip")
    ys = _experts(xs, tile_expert, n_used, w_gate, w_up, w_down, j, tm_exp, tf)
    y0 = jnp.take(ys, slot[:, 0], axis=0, mode="clip")
    y1 = jnp.take(ys, slot[:, 1], axis=0, mode="clip")
    gn = final_g if final_g is not None else jnp.ones((x2.shape[1],), F32)
    return _combine(x2, y0, y1, wgt, gn, tm_tok, final_g is not None)


def _final_norm_kernel(x_ref, g_ref, o_ref):
    o_ref[...] = _rms(x_ref[...], g_ref[...])


def _final_norm(x2, g, tm):
    N, D = x2.shape
    return pl.pallas_call(
        _final_norm_kernel,
        grid=(N // tm,),
        in_specs=[pl.BlockSpec((tm, D), lambda i: (i, 0)), _resident((1, D))],
        out_specs=pl.BlockSpec((tm, D), lambda i: (i, 0)),
        out_shape=jax.ShapeDtypeStruct((N, D), F32),
        compiler_params=_cparams("parallel"),
        name="final_norm",
    )(x2, g.reshape(1, D))


def _tile(n, pref):
    t = min(n, pref)
    while n % t:
        t //= 2
    return t


def kernel(x, mix_norm_g, w_in, gate_b, s5_lambda_re, s5_lambda_im, s5_log_dt, s5_b_re, s5_b_im, s5_c_re, s5_c_im, s5_d, s5_w_glu, s5_b_glu, conv_w, conv_b, lru_w_a, lru_b_a, lru_w_x, lru_b_x, lru_lambda, w_branch, w_out, ffn_norm_g, ffn_w_gate, ffn_w_up, ffn_w_down, router_w, router_b, moe_w_gate, moe_w_up, moe_w_down, final_norm_g):
    B, T, D = x.shape
    depth = w_in.shape[0]
    W = MIX_WIDTH
    N = B * T
    tm = _tile(T, 512)
    tq = _tile(T, 256)
    tt = _tile(T, 128)
    L = _tile(T, S5_CHUNK)
    tm_exp = _tile(N, 1024)
    x = x.astype(F32)
    w_in_bf = w_in.astype(BF16)
    for layer in range(depth):
        main, xl_tm, yl_tm, gates = _norm_proj(x, mix_norm_g[layer], w_in_bf, layer, tm)
        consts = _s5_consts(s5_lambda_re[layer], s5_lambda_im[layer], s5_log_dt[layer],
                            s5_b_re[layer], s5_b_im[layer], s5_c_re[layer], s5_c_im[layer], L)
        s5 = _s5(main, consts, s5_d[layer], s5_w_glu[layer].astype(BF16), s5_b_glu[layer], tm, L)
        attn = _attention(main, tq)
        lru = _rglru(xl_tm.reshape(T * B, W), yl_tm.reshape(T * B, W), conv_w[layer], conv_b[layer],
                     lru_w_a[layer], lru_b_a[layer], lru_w_x[layer], lru_b_x[layer],
                     lru_lambda[layer], B, tt)
        x = _merge(s5, attn, lru.reshape(T, B * W), gates, x, w_branch[layer].astype(BF16),
                   gate_b[layer], w_out[layer].astype(BF16), tm)
        x2 = x.reshape(N, D)
        last = layer == depth - 1
        j = layer // 2
        if layer % 2 == 0:
            x2 = _ffn(x2, ffn_norm_g[layer], ffn_w_gate[j].astype(BF16), ffn_w_up[j].astype(BF16),
                      ffn_w_down[j].astype(BF16), tm, ffn_w_gate.shape[2] // 2)
            if last:
                x2 = _final_norm(x2, final_norm_g, tm)
        else:
            x2 = _moe(x2, ffn_norm_g[layer], router_w[j], router_b[j], moe_w_gate, moe_w_up,
                      moe_w_down, j, final_norm_g if last else None, tm, tm_exp, 512)
        x = x2.reshape(B, T, D)
    return x
```

```python
import functools
import math

import jax
import jax.numpy as jnp
from jax import lax
from jax.experimental import pallas as pl
from jax.experimental.pallas import tpu as pltpu

F32 = jnp.float32
BF16 = jnp.bfloat16

RMS_EPS = 1e-6
MIX_WIDTH = 384
S5_GROUP = 16
S5_STATE = 64
S5_CHUNK = 128
SB_HEAD_DIM = 64
LRU_BLOCKS = 6
CONV_WIDTH = 4
LRU_C = 8.0
N_BRANCH = 3
TOP_K = 2
LANES = 128
NEG_BIG = -1e30
LOG2E = 1.4426950408889634
POW2_ZERO_BELOW = -150.0
VMEM_LIMIT = 56 * 1024 * 1024


def _cparams(*sem):
    return pltpu.CompilerParams(dimension_semantics=sem, vmem_limit_bytes=VMEM_LIMIT)


def _resident(shape):
    n = len(shape)
    return pl.BlockSpec(shape, lambda *_: (0,) * n, pipeline_mode=pl.Buffered(1))


def _rms(xf, g):
    return xf * lax.rsqrt(jnp.mean(xf * xf, axis=-1, keepdims=True) + RMS_EPS) * g


def _gelu(x):
    c = math.sqrt(2.0 / math.pi)
    return 0.5 * x * (1.0 + jnp.tanh(c * (x + 0.044715 * (x * x * x))))


def _sigmoid(x):
    return 1.0 / (1.0 + jnp.exp(-x))


def _dot(a, b):
    return jnp.dot(a, b, preferred_element_type=F32)


def _norm_proj_kernel(x_ref, g_ref, w_ref, main_ref, gate_ref, *, w):
    h = _rms(x_ref[...], g_ref[...]).astype(BF16)
    half = 3 * w
    main_ref[:, 0:half] = _dot(h, w_ref[:, 0:half]).astype(BF16)
    main_ref[:, half:2 * half] = _dot(h, w_ref[:, half:2 * half]).astype(BF16)
    d = gate_ref.shape[-1] // N_BRANCH
    for n in range(N_BRANCH):
        c0 = 6 * w + n * d
        gate_ref[:, n * d:(n + 1) * d] = _dot(h, w_ref[:, c0:c0 + d]).astype(BF16)


def _norm_proj(x, g, w_in, layer, tm):
    B, T, D = x.shape
    W = MIX_WIDTH
    return pl.pallas_call(
        functools.partial(_norm_proj_kernel, w=W),
        grid=(B, T // tm),
        in_specs=[pl.BlockSpec((None, tm, D), lambda b, i: (b, i, 0)),
                  _resident((1, D)),
                  pl.BlockSpec((None,) + w_in.shape[1:], lambda b, i: (layer, 0, 0),
                               pipeline_mode=pl.Buffered(1))],
        out_specs=[pl.BlockSpec((None, tm, 6 * W), lambda b, i: (b, i, 0)),
                   pl.BlockSpec((None, tm, N_BRANCH * D), lambda b, i: (b, i, 0))],
        out_shape=[jax.ShapeDtypeStruct((B, T, 6 * W), BF16),
                   jax.ShapeDtypeStruct((B, T, N_BRANCH * D), BF16)],
        compiler_params=_cparams("parallel", "parallel"),
        name="norm_proj",
    )(x, g.reshape(1, D), w_in)


def _s5_consts(lam_re, lam_im, log_dt, b_re, b_im, c_re, c_im, L):
    G, P, H = b_re.shape
    ns = (G * H) // LANES
    gs = G // ns
    dt = jnp.exp(log_dt)[:, None]
    ea = jnp.exp(lam_re * dt)
    abar_re = ea * jnp.cos(lam_im * dt)
    abar_im = ea * jnp.sin(lam_im * dt)
    den = lam_re * lam_re + lam_im * lam_im
    nr = abar_re - 1.0
    coef_re = (nr * lam_re + abar_im * lam_im) / den
    coef_im = (abar_im * lam_re - nr * lam_im) / den
    bb_re = coef_re[..., None] * b_re - coef_im[..., None] * b_im
    bb_im = coef_re[..., None] * b_im + coef_im[..., None] * b_re
    eye = jnp.eye(gs, dtype=F32)

    def in_slab(bb):
        return jnp.einsum('sgph,gk->sghkp', bb.reshape(ns, gs, P, H), eye).reshape(ns, gs * H, gs * P)

    def out_slab(c):
        return jnp.einsum('sgop,gk->sgpko', c.reshape(ns, gs, H, P), eye).reshape(ns, gs * P, gs * H)

    b_slab = jnp.concatenate([in_slab(bb_re), in_slab(bb_im)], axis=-1).astype(BF16)
    c_slab = jnp.concatenate([out_slab(c_re), -out_slab(c_im)], axis=1).astype(BF16)

    def power(k):
        mag = jnp.exp((lam_re * dt)[None] * k[:, None, None])
        ang = (lam_im * dt)[None] * k[:, None, None]
        lay = lambda v: v.reshape(k.shape[0], ns, gs * P).transpose(1, 0, 2)
        return lay(mag * jnp.cos(ang)), lay(mag * jnp.sin(ang))

    t = jnp.arange(L, dtype=F32)
    m = float(L // 2)
    em_re, em_im = power(m - t)
    ep_re, ep_im = power(t - m)
    vf_re, vf_im = power(jnp.full((1,), m + 1.0, F32))
    return b_slab, c_slab, em_re, em_im, ep_re, ep_im, vf_re, vf_im


def _s5_kernel(u_ref, b_ref, c_ref, emr_ref, emi_ref, epr_ref, epi_ref, vfr_ref, vfi_ref,
               d_ref, wglu_ref, bglu_ref, o_ref, carry_scr, s_scr, *, L):
    tc = u_ref.shape[0]
    ns = b_ref.shape[0]
    half = b_ref.shape[2] // 2

    @pl.when(pl.program_id(1) == 0)
    def _():
        carry_scr[...] = jnp.zeros_like(carry_scr)

    row = lax.broadcasted_iota(jnp.int32, (L, L), 0)
    col = lax.broadcasted_iota(jnp.int32, (L, L), 1)
    tri = (col <= row).astype(BF16)
    ys = []
    for j in range(ns):
        u = u_ref[:, j * LANES:(j + 1) * LANES]
        bu = _dot(u, b_ref[j])
        emr, emi, epr, epi = emr_ref[j], emi_ref[j], epr_ref[j], epi_ref[j]
        vfr, vfi = vfr_ref[j], vfi_ref[j]
        c_re = carry_scr[2 * j:2 * j + 1, :]
        c_im = carry_scr[2 * j + 1:2 * j + 2, :]
        for c in range(tc // L):
            bur = bu[c * L:(c + 1) * L, :half]
            bui = bu[c * L:(c + 1) * L, half:]
            scaled = jnp.concatenate([bur * emr - bui * emi, bur * emi + bui * emr], axis=1)
            pre = _dot(tri, scaled.astype(BF16))
            pr = pre[:, :half] + (vfr * c_re - vfi * c_im)
            pi = pre[:, half:] + (vfr * c_im + vfi * c_re)
            s_re = pr * epr - pi * epi
            s_im = pr * epi + pi * epr
            c_re = s_re[L - 1:L, :]
            c_im = s_im[L - 1:L, :]
            s_scr[c * L:(c + 1) * L, :] = jnp.concatenate([s_re, s_im], axis=1).astype(BF16)
        carry_scr[2 * j:2 * j + 1, :] = c_re
        carry_scr[2 * j + 1:2 * j + 2, :] = c_im
        ys.append(_dot(s_scr[...], c_ref[j]))
    y = jnp.concatenate(ys, axis=1) + d_ref[...] * u_ref[...].astype(F32)
    ya = _gelu(y)
    o_ref[...] = (ya * _sigmoid(_dot(ya.astype(BF16), wglu_ref[...]) + bglu_ref[...])).astype(BF16)


def _s5(main, consts, d, w_glu, b_glu, tc, L):
    B, T, _ = main.shape
    W = MIX_WIDTH
    b_slab, c_slab = consts[0], consts[1]
    ns, _, two_half = b_slab.shape
    return pl.pallas_call(
        functools.partial(_s5_kernel, L=L),
        grid=(B, T // tc),
        in_specs=[pl.BlockSpec((None, tc, W), lambda b, i: (b, i, 0))]
        + [_resident(c.shape) for c in consts]
        + [_resident((1, W)), _resident((W, W)), _resident((1, W))],
        out_specs=pl.BlockSpec((None, tc, W), lambda b, i: (b, i, 0)),
        out_shape=jax.ShapeDtypeStruct((B, T, W), BF16),
        scratch_shapes=[pltpu.VMEM((2 * ns, two_half // 2), F32),
                        pltpu.VMEM((tc, two_half), BF16)],
        compiler_params=_cparams("parallel", "arbitrary"),
        name="s5_mixer",
    )(main, *consts, d.reshape(1, W), w_glu, b_glu.reshape(1, W))


def _attn_kernel(q_ref, k_ref, v_ref, o_ref, qs_scr, acc_scr, run_scr, *, tq, heads, dh):
    i = pl.program_id(1)
    row = lax.broadcasted_iota(jnp.int32, (tq, tq), 0)
    col = lax.broadcasted_iota(jnp.int32, (tq, tq), 1)
    suffix = (row > col).astype(BF16)
    causal = col < row
    nt = (((1,), (1,)), ((), ()))
    qs_scr[...] = (q_ref[...].astype(F32) * (dh ** -0.5 * LOG2E)).astype(BF16)

    def logs(h, r0):
        hs = slice(h * dh, (h + 1) * dh)
        z = lax.dot_general(qs_scr[:, hs], k_ref[pl.ds(r0, tq), hs], nt, preferred_element_type=F32)
        sp = jnp.log2(1.0 + jnp.exp2(-jnp.abs(z)))
        lb = jnp.minimum(z, 0.0) - sp
        return lb, lb - z

    r_diag = pl.multiple_of(i * tq, tq)
    run_scr[...] = jnp.full(run_scr.shape, NEG_BIG, F32)
    for h in range(heads):
        hs = slice(h * dh, (h + 1) * dh)
        lb, l1 = logs(h, r_diag)
        l1 = jnp.where(causal, l1, 0.0)
        cum = _dot(l1.astype(BF16), suffix)
        wgt = jnp.where(causal, jnp.exp2(lb + cum), 0.0)
        acc_scr[:, hs] = _dot(wgt.astype(BF16), v_ref[pl.ds(r_diag, tq), hs])
        run_scr[:, h:h + 1] = cum[:, 0:1] + l1[:, 0:1]

    def cond(carry):
        j, top = carry
        return jnp.logical_and(j < i, top > POW2_ZERO_BELOW)

    def body(carry):
        j, _ = carry
        r0 = pl.multiple_of((i - 1 - j) * tq, tq)
        for h in range(heads):
            hs = slice(h * dh, (h + 1) * dh)
            run = run_scr[:, h:h + 1]
            lb, l1 = logs(h, r0)
            cum = _dot(l1.astype(BF16), suffix)
            wgt = jnp.exp2(lb + cum + run)
            acc_scr[:, hs] += _dot(wgt.astype(BF16), v_ref[pl.ds(r0, tq), hs])
            run_scr[:, h:h + 1] = run + cum[:, 0:1] + l1[:, 0:1]
        return j + 1, jnp.max(run_scr[...])

    lax.while_loop(cond, body, (jnp.int32(0), jnp.max(run_scr[...])))
    o_ref[...] = acc_scr[...].astype(BF16)


def _attention(main, tq):
    B, T, _ = main.shape
    W = MIX_WIDTH
    heads = W // SB_HEAD_DIM
    return pl.pallas_call(
        functools.partial(_attn_kernel, tq=tq, heads=heads, dh=SB_HEAD_DIM),
        grid=(B, T // tq),
        in_specs=[pl.BlockSpec((None, tq, W), lambda b, i: (b, i, 1)),
                  pl.BlockSpec((None, T, W), lambda b, i: (b, 0, 2)),
                  pl.BlockSpec((None, T, W), lambda b, i: (b, 0, 3))],
        out_specs=pl.BlockSpec((None, tq, W), lambda b, i: (b, i, 0)),
        out_shape=jax.ShapeDtypeStruct((B, T, W), BF16),
        scratch_shapes=[pltpu.VMEM((tq, W), BF16),
                        pltpu.VMEM((tq, W), F32),
                        pltpu.VMEM((tq, LANES), F32)],
        compiler_params=_cparams("parallel", "arbitrary"),
        name="sb_attention",
    )(main, main, main)


def _lru_kernel(x_ref, y_ref, cw_ref, cb_ref, wa_ref, ba_ref, wx_ref, bx_ref, lam_ref, o_ref,
                tail_scr, h_scr, a_scr, b_scr, *, pitch):
    nb, tt, w = x_ref.shape
    ns = w // LANES
    halo = tail_scr.shape[1]

    @pl.when(pl.program_id(0) == 0)
    def _():
        tail_scr[...] = jnp.zeros_like(tail_scr)
        h_scr[...] = jnp.zeros_like(h_scr)

    x = x_ref[...].astype(F32)
    xx = jnp.concatenate([tail_scr[...], x], axis=1)
    tail_scr[...] = x[:, tt - halo:, :]
    xc = cb_ref[...]
    for j in range(CONV_WIDTH):
        off = halo - (CONV_WIDTH - 1) + j
        xc = xc + cw_ref[j:j + 1, :] * xx[:, off:off + tt, :]
    xc = xc.reshape(nb * tt, w)
    xcb = xc.astype(BF16)
    r = _sigmoid(_dot(xcb, wa_ref[...]) + ba_ref[...])
    ig = _sigmoid(_dot(xcb, wx_ref[...]) + bx_ref[...])
    lam = lam_ref[...]
    log_sig_lam = jnp.minimum(lam, 0.0) - jnp.log(1.0 + jnp.exp(-jnp.abs(lam)))
    log_a = LRU_C * r * log_sig_lam
    a = jnp.exp(log_a)
    bb = jnp.sqrt(1.0 - a * a) * (ig * xc)
    for b in range(nb):
        for s in range(ns):
            a_scr[s, b * pitch:b * pitch + tt, :] = a[b * tt:(b + 1) * tt, s * LANES:(s + 1) * LANES]
            b_scr[s, b * pitch:b * pitch + tt, :] = bb[b * tt:(b + 1) * tt, s * LANES:(s + 1) * LANES]

    def step(t, h):
        rows_t = pl.ds(t, nb, stride=pitch)
        out = []
        for s in range(ns):
            hs = a_scr[s, rows_t, :] * h[s] + b_scr[s, rows_t, :]
            b_scr[s, rows_t, :] = hs
            out.append(hs)
        return tuple(out)

    h = lax.fori_loop(0, tt, step, tuple(h_scr[s] for s in range(ns)), unroll=8)
    for s in range(ns):
        h_scr[s] = h[s]
    for b in range(nb):
        for s in range(ns):
            cols = slice(s * LANES, (s + 1) * LANES)
            hb = b_scr[s, b * pitch:b * pitch + tt, :]
            o_ref[b, :, cols] = (hb * _gelu(y_ref[b, :, cols].astype(F32))).astype(BF16)


def _block_diag(w):
    n, k, _ = w.shape
    eye = jnp.eye(n, dtype=w.dtype)
    return (eye[:, None, :, None] * w[:, :, None, :]).reshape(n * k, n * k)


def _rglru(main, conv_w, conv_b, w_a, b_a, w_x, b_x, lam, tt):
    B, T, _ = main.shape
    W = MIX_WIDTH
    halo = 8
    pitch = tt + 8
    row = lambda v: v.reshape(1, W)
    return pl.pallas_call(
        functools.partial(_lru_kernel, pitch=pitch),
        grid=(T // tt,),
        in_specs=[pl.BlockSpec((B, tt, W), lambda i: (0, i, 4)),
                  pl.BlockSpec((B, tt, W), lambda i: (0, i, 5)),
                  _resident((CONV_WIDTH, W)), _resident((1, W)),
                  _resident((W, W)), _resident((1, W)),
                  _resident((W, W)), _resident((1, W)), _resident((1, W))],
        out_specs=pl.BlockSpec((B, tt, W), lambda i: (0, i, 0)),
        out_shape=jax.ShapeDtypeStruct((B, T, W), BF16),
        scratch_shapes=[pltpu.VMEM((B, halo, W), F32),
                        pltpu.VMEM((W // LANES, B, LANES), F32),
                        pltpu.VMEM((W // LANES, B * pitch, LANES), F32),
                        pltpu.VMEM((W // LANES, B * pitch, LANES), F32)],
        compiler_params=_cparams("arbitrary"),
        name="rglru",
    )(main, main, conv_w, row(conv_b), _block_diag(w_a).astype(BF16), row(b_a),
      _block_diag(w_x).astype(BF16), row(b_x), row(lam))


def _merge_kernel(s5_ref, at_ref, lr_ref, gate_ref, x_ref, wbr_ref, gb_ref, wout_ref, o_ref):
    dm = x_ref.shape[-1]
    branches = (s5_ref[...], at_ref[...], lr_ref[...])
    merged = None
    for n in range(N_BRANCH):
        gate = _sigmoid(gate_ref[:, n * dm:(n + 1) * dm].astype(F32) + gb_ref[:, n * dm:(n + 1) * dm])
        term = gate * _dot(branches[n], wbr_ref[n])
        merged = term if merged is None else merged + term
    o_ref[...] = x_ref[...] + _dot(merged.astype(BF16), wout_ref[...])


def _merge(s5, attn, lru, gates, x, w_branch, gate_b, w_out, tm):
    B, T, D = x.shape
    W = MIX_WIDTH
    tok = lambda width: pl.BlockSpec((None, tm, width), lambda b, i: (b, i, 0))
    return pl.pallas_call(
        _merge_kernel,
        grid=(B, T // tm),
        in_specs=[tok(W), tok(W), tok(W),
                  tok(N_BRANCH * D), tok(D),
                  _resident((N_BRANCH, W, D)), _resident((1, N_BRANCH * D)), _resident((D, D))],
        out_specs=tok(D),
        out_shape=jax.ShapeDtypeStruct((B, T, D), F32),
        compiler_params=_cparams("parallel", "parallel"),
        name="merge_out",
    )(s5, attn, lru, gates, x, w_branch, gate_b.reshape(1, N_BRANCH * D), w_out)


def _ffn_kernel(x_ref, g_ref, wg_ref, wu_ref, wd_ref, o_ref, *, tf):
    x = x_ref[...]
    h = _rms(x, g_ref[...]).astype(BF16)
    acc = x
    for c0 in range(0, wg_ref.shape[1], tf):
        gate = _dot(h, wg_ref[:, c0:c0 + tf])
        up = _dot(h, wu_ref[:, c0:c0 + tf])
        act = (gate * _sigmoid(gate) * up).astype(BF16)
        acc = acc + _dot(act, wd_ref[c0:c0 + tf, :])
    o_ref[...] = acc


def _ffn(x2, g, w_gate, w_up, w_down, tm, tf):
    N, D = x2.shape
    F = w_gate.shape[1]
    return pl.pallas_call(
        functools.partial(_ffn_kernel, tf=tf),
        grid=(N // tm,),
        in_specs=[pl.BlockSpec((tm, D), lambda i: (i, 0)), _resident((1, D)),
                  _resident((D, F)), _resident((D, F)), _resident((F, D))],
        out_specs=pl.BlockSpec((tm, D), lambda i: (i, 0)),
        out_shape=jax.ShapeDtypeStruct((N, D), F32),
        compiler_params=_cparams("parallel"),
        name="ffn_swiglu",
    )(x2, g.reshape(1, D), w_gate, w_up, w_down)


def _router_kernel(x_ref, g_ref, rw_ref, rb_ref, h_ref, idx_ref, wgt_ref, *, n_exp):
    h = _rms(x_ref[...], g_ref[...])
    h_ref[...] = h.astype(BF16)
    logits = jnp.dot(h, rw_ref[...], preferred_element_type=F32, precision=lax.Precision.HIGHEST)
    lane = lax.broadcasted_iota(jnp.int32, logits.shape, 1)
    lg = jnp.where(lane < n_exp, logits + rb_ref[...], NEG_BIG)
    m1 = jnp.max(lg, axis=1, keepdims=True)
    i1 = jnp.min(jnp.where(lg == m1, lane, LANES), axis=1, keepdims=True)
    lg2 = jnp.where(lane == i1, NEG_BIG, lg)
    m2 = jnp.max(lg2, axis=1, keepdims=True)
    i2 = jnp.min(jnp.where(lg2 == m2, lane, LANES), axis=1, keepdims=True)
    e = jnp.exp(m2 - m1)
    w1 = 1.0 / (1.0 + e)
    w2 = e / (1.0 + e)
    idx_ref[...] = jnp.where(lane == 0, i1, jnp.where(lane == 1, i2, 0))
    wgt_ref[...] = jnp.where(lane == 0, w1, jnp.where(lane == 1, w2, 0.0))


def _router(x2, g, router_w, router_b, tm):
    N, D = x2.shape
    E = router_w.shape[1]
    rw = jnp.zeros((D, LANES), F32).at[:, :E].set(router_w)
    rb = jnp.zeros((1, LANES), F32).at[0, :E].set(router_b)
    return pl.pallas_call(
        functools.partial(_router_kernel, n_exp=E),
        grid=(N // tm,),
        in_specs=[pl.BlockSpec((tm, D), lambda i: (i, 0)), _resident((1, D)),
                  _resident((D, LANES)), _resident((1, LANES))],
        out_specs=[pl.BlockSpec((tm, D), lambda i: (i, 0)),
                   pl.BlockSpec((tm, LANES), lambda i: (i, 0)),
                   pl.BlockSpec((tm, LANES), lambda i: (i, 0))],
        out_shape=[jax.ShapeDtypeStruct((N, D), BF16),
                   jax.ShapeDtypeStruct((N, LANES), jnp.int32),
                   jax.ShapeDtypeStruct((N, LANES), F32)],
        compiler_params=_cparams("parallel"),
        name="moe_router",
    )(x2, g.reshape(1, D), rw, rb)


def _expert_kernel(te_ref, nu_ref, xs_ref, wg_ref, wu_ref, wd_ref, o_ref, acc_ref):
    t = pl.program_id(0)
    f = pl.program_id(1)
    used = t < nu_ref[0]

    @pl.when(f == 0)
    def _():
        acc_ref[...] = jnp.zeros_like(acc_ref)

    @pl.when(used)
    def _():
        xs = xs_ref[...]
        gate = _dot(xs, wg_ref[...].astype(BF16))
        up = _dot(xs, wu_ref[...].astype(BF16))
        act = (gate * _sigmoid(gate) * up).astype(BF16)
        acc_ref[...] += _dot(act, wd_ref[...].astype(BF16))

    @pl.when(f == pl.num_programs(1) - 1)
    def _():
        o_ref[...] = acc_ref[...].astype(BF16)


def _experts(xs, tile_expert, n_used, w_gate, w_up, w_down, j, tm, tf):
    S, D = xs.shape
    F = w_gate.shape[3]
    return pl.pallas_call(
        _expert_kernel,
        grid_spec=pltpu.PrefetchScalarGridSpec(
            num_scalar_prefetch=2,
            grid=(S // tm, F // tf),
            in_specs=[pl.BlockSpec((tm, D), lambda t, f, te, nu: (t, 0)),
                      pl.BlockSpec((None, None, D, tf), lambda t, f, te, nu: (j, te[t], 0, f)),
                      pl.BlockSpec((None, None, D, tf), lambda t, f, te, nu: (j, te[t], 0, f)),
                      pl.BlockSpec((None, None, tf, D), lambda t, f, te, nu: (j, te[t], f, 0))],
            out_specs=pl.BlockSpec((tm, D), lambda t, f, te, nu: (t, 0)),
            scratch_shapes=[pltpu.VMEM((tm, D), F32)]),
        out_shape=jax.ShapeDtypeStruct((S, D), BF16),
        compiler_params=_cparams("parallel", "arbitrary"),
        name="moe_experts",
    )(tile_expert, n_used, xs, w_gate, w_up, w_down)


def _route(idx, n_exp, tm):
    N, K = idx.shape
    flat = idx.reshape(-1)
    onehot = (flat[:, None] == jnp.arange(n_exp)[None, :]).astype(jnp.int32)
    rank = jnp.cumsum(onehot, axis=0) - onehot
    counts = jnp.sum(onehot, axis=0)
    padded = ((counts + tm - 1) // tm) * tm
    ends = jnp.cumsum(padded)
    starts = ends - padded
    slot = jnp.sum((starts[None, :] + rank) * onehot, axis=1)
    S = N * K + n_exp * tm
    tile_start = jnp.arange(S // tm, dtype=jnp.int32) * tm
    last_start = jnp.maximum(ends[-1] - tm, 0)
    tile_expert = jnp.sum((jnp.minimum(tile_start, last_start)[:, None] >= ends[None, :])
                          .astype(jnp.int32), axis=1)
    tile_expert = jnp.minimum(tile_expert, n_exp - 1)
    n_used = (ends[-1] // tm).astype(jnp.int32).reshape(1)
    order = jnp.argsort(flat, stable=True).astype(jnp.int32)
    per_slot = lambda v: jnp.repeat(v[tile_expert], tm)
    within = jnp.arange(S, dtype=jnp.int32) - per_slot(starts)
    compact = jnp.clip(per_slot(jnp.cumsum(counts) - counts) + within, 0, N * K - 1)
    src = jnp.where(within < per_slot(counts), order[compact] // K, jnp.arange(S, dtype=jnp.int32) % N)
    return src, slot.reshape(N, K), tile_expert, n_used


def _combine_kernel(x_ref, y0_ref, y1_ref, wgt_ref, g_ref, o_ref, *, final_norm):
    w0 = wgt_ref[:, 0:1]
    w1 = wgt_ref[:, 1:2]
    x = x_ref[...] + w0 * y0_ref[...].astype(F32) + w1 * y1_ref[...].astype(F32)
    o_ref[...] = _rms(x, g_ref[...]) if final_norm else x


def _combine(x2, y0, y1, wgt, g, tm, final_norm):
    N, D = x2.shape
    return pl.pallas_call(
        functools.partial(_combine_kernel, final_norm=final_norm),
        grid=(N // tm,),
        in_specs=[pl.BlockSpec((tm, D), lambda i: (i, 0)),
                  pl.BlockSpec((tm, D), lambda i: (i, 0)),
                  pl.BlockSpec((tm, D), lambda i: (i, 0)),
                  pl.BlockSpec((tm, LANES), lambda i: (i, 0)),
                  _resident((1, D))],
        out_specs=pl.BlockSpec((tm, D), lambda i: (i, 0)),
        out_shape=jax.ShapeDtypeStruct((N, D), F32),
        compiler_params=_cparams("parallel"),
        name="moe_combine",
    )(x2, y0, y1, wgt, g.reshape(1, D))


def _moe(x2, g, router_w, router_b, w_gate, w_up, w_down, j, final_g, tm_tok, tm_exp, tf):
    n_exp = router_w.shape[1]
    h, idx, wgt = _router(x2, g, router_w, router_b, tm_tok)
    src, slot, tile_expert, n_used = _route(idx[:, :TOP_K], n_exp, tm_exp)
    xs = jnp.take(h, src, axis=0, mode="clip")
    ys = _experts(xs, tile_expert, n_used, w_gate, w_up, w_down, j, tm_exp, tf)
    y0 = jnp.take(ys, slot[:, 0], axis=0, mode="clip")
    y1 = jnp.take(ys, slot[:, 1], axis=0, mode="clip")
    gn = final_g if final_g is not None else jnp.ones((x2.shape[1],), F32)
    return _combine(x2, y0, y1, wgt, gn, tm_tok, final_g is not None)


def _final_norm_kernel(x_ref, g_ref, o_ref):
    o_ref[...] = _rms(x_ref[...], g_ref[...])


def _final_norm(x2, g, tm):
    N, D = x2.shape
    return pl.pallas_call(
        _final_norm_kernel,
        grid=(N // tm,),
        in_specs=[pl.BlockSpec((tm, D), lambda i: (i, 0)), _resident((1, D))],
        out_specs=pl.BlockSpec((tm, D), lambda i: (i, 0)),
        out_shape=jax.ShapeDtypeStruct((N, D), F32),
        compiler_params=_cparams("parallel"),
        name="final_norm",
    )(x2, g.reshape(1, D))


def _tile(n, pref):
    t = min(n, pref)
    while n % t:
        t //= 2
    return t


def kernel(x, mix_norm_g, w_in, gate_b, s5_lambda_re, s5_lambda_im, s5_log_dt, s5_b_re, s5_b_im, s5_c_re, s5_c_im, s5_d, s5_w_glu, s5_b_glu, conv_w, conv_b, lru_w_a, lru_b_a, lru_w_x, lru_b_x, lru_lambda, w_branch, w_out, ffn_norm_g, ffn_w_gate, ffn_w_up, ffn_w_down, router_w, router_b, moe_w_gate, moe_w_up, moe_w_down, final_norm_g):
    B, T, D = x.shape
    depth = w_in.shape[0]
    W = MIX_WIDTH
    N = B * T
    tm = _tile(T, 512)
    tq = _tile(T, 256)
    tt = _tile(T, 128)
    L = _tile(T, S5_CHUNK)
    tm_exp = _tile(N, 1024)
    x = x.astype(F32)
    w_in_bf = w_in.astype(BF16)
    for layer in range(depth):
        main, gates = _norm_proj(x, mix_norm_g[layer], w_in_bf, layer, tm)
        consts = _s5_consts(s5_lambda_re[layer], s5_lambda_im[layer], s5_log_dt[layer],
                            s5_b_re[layer], s5_b_im[layer], s5_c_re[layer], s5_c_im[layer], L)
        s5 = _s5(main, consts, s5_d[layer], s5_w_glu[layer].astype(BF16), s5_b_glu[layer], tm, L)
        attn = _attention(main, tq)
        lru = _rglru(main, conv_w[layer], conv_b[layer], lru_w_a[layer], lru_b_a[layer],
                     lru_w_x[layer], lru_b_x[layer], lru_lambda[layer], tt)
        x = _merge(s5, attn, lru, gates, x, w_branch[layer].astype(BF16),
                   gate_b[layer], w_out[layer].astype(BF16), tm)
        x2 = x.reshape(N, D)
        last = layer == depth - 1
        j = layer // 2
        if layer % 2 == 0:
            x2 = _ffn(x2, ffn_norm_g[layer], ffn_w_gate[j].astype(BF16), ffn_w_up[j].astype(BF16),
                      ffn_w_down[j].astype(BF16), tm, ffn_w_gate.shape[2] // 2)
            if last:
                x2 = _final_norm(x2, final_norm_g, tm)
        else:
            x2 = _moe(x2, ffn_norm_g[layer], router_w[j], router_b[j], moe_w_gate, moe_w_up,
                      moe_w_down, j, final_norm_g if last else None, tm, tm_exp, 512)
        x = x2.reshape(B, T, D)
    return x
```

```python
import functools
import math

import jax
import jax.numpy as jnp
from jax import lax
from jax.experimental import pallas as pl
from jax.experimental.pallas import tpu as pltpu

F32 = jnp.float32
BF16 = jnp.bfloat16

RMS_EPS = 1e-6
MIX_WIDTH = 384
S5_GROUP = 16
S5_STATE = 64
S5_CHUNK = 128
SB_HEAD_DIM = 64
LRU_BLOCKS = 6
CONV_WIDTH = 4
LRU_C = 8.0
N_BRANCH = 3
TOP_K = 2
LANES = 128
NEG_BIG = -1e30
LOG2E = 1.4426950408889634
POW2_ZERO_BELOW = -150.0
VMEM_LIMIT = 56 * 1024 * 1024


def _cparams(*sem):
    return pltpu.CompilerParams(dimension_semantics=sem, vmem_limit_bytes=VMEM_LIMIT)


def _resident(shape):
    n = len(shape)
    return pl.BlockSpec(shape, lambda *_: (0,) * n, pipeline_mode=pl.Buffered(1))


def _rms(xf, g):
    return xf * lax.rsqrt(jnp.mean(xf * xf, axis=-1, keepdims=True) + RMS_EPS) * g


def _gelu(x):
    c = math.sqrt(2.0 / math.pi)
    return 0.5 * x * (1.0 + jnp.tanh(c * (x + 0.044715 * (x * x * x))))


def _sigmoid(x):
    return 1.0 / (1.0 + jnp.exp(-x))


def _dot(a, b):
    return jnp.dot(a, b, preferred_element_type=F32)


def _norm_proj_kernel(x_ref, g_ref, w_ref, main_ref, kt_ref, gate_ref, *, w):
    h = _rms(x_ref[...], g_ref[...]).astype(BF16)
    half = 3 * w
    uqk = _dot(h, w_ref[:, 0:half])
    main_ref[:, 0:half] = uqk.astype(BF16)
    main_ref[:, half:2 * half] = _dot(h, w_ref[:, half:2 * half]).astype(BF16)
    kt = uqk[:, 2 * w:3 * w].T
    kb = kt_ref.shape[-1]
    for c in range(kt_ref.shape[0]):
        kt_ref[c] = kt[:, c * kb:(c + 1) * kb].astype(BF16)
    d = gate_ref.shape[-1] // N_BRANCH
    for n in range(N_BRANCH):
        c0 = 6 * w + n * d
        gate_ref[:, n * d:(n + 1) * d] = _dot(h, w_ref[:, c0:c0 + d]).astype(BF16)


def _norm_proj(x, g, w_in, layer, tm, kb):
    B, T, D = x.shape
    W = MIX_WIDTH
    nk = tm // kb
    return pl.pallas_call(
        functools.partial(_norm_proj_kernel, w=W),
        grid=(B, T // tm),
        in_specs=[pl.BlockSpec((None, tm, D), lambda b, i: (b, i, 0)),
                  _resident((1, D)),
                  pl.BlockSpec((None,) + w_in.shape[1:], lambda b, i: (layer, 0, 0),
                               pipeline_mode=pl.Buffered(1))],
        out_specs=[pl.BlockSpec((None, tm, 6 * W), lambda b, i: (b, i, 0)),
                   pl.BlockSpec((None, nk, W, kb), lambda b, i: (b, i, 0, 0)),
                   pl.BlockSpec((None, tm, N_BRANCH * D), lambda b, i: (b, i, 0))],
        out_shape=[jax.ShapeDtypeStruct((B, T, 6 * W), BF16),
                   jax.ShapeDtypeStruct((B, T // kb, W, kb), BF16),
                   jax.ShapeDtypeStruct((B, T, N_BRANCH * D), BF16)],
        compiler_params=_cparams("parallel", "parallel"),
        name="norm_proj",
    )(x, g.reshape(1, D), w_in)


def _s5_consts(lam_re, lam_im, log_dt, b_re, b_im, c_re, c_im, L):
    G, P, H = b_re.shape
    ns = (G * H) // LANES
    gs = G // ns
    dt = jnp.exp(log_dt)[:, None]
    ea = jnp.exp(lam_re * dt)
    abar_re = ea * jnp.cos(lam_im * dt)
    abar_im = ea * jnp.sin(lam_im * dt)
    den = lam_re * lam_re + lam_im * lam_im
    nr = abar_re - 1.0
    coef_re = (nr * lam_re + abar_im * lam_im) / den
    coef_im = (abar_im * lam_re - nr * lam_im) / den
    bb_re = coef_re[..., None] * b_re - coef_im[..., None] * b_im
    bb_im = coef_re[..., None] * b_im + coef_im[..., None] * b_re
    eye = jnp.eye(gs, dtype=F32)

    def in_slab(bb):
        return jnp.einsum('sgph,gk->sghkp', bb.reshape(ns, gs, P, H), eye).reshape(ns, gs * H, gs * P)

    def out_slab(c):
        return jnp.einsum('sgop,gk->sgpko', c.reshape(ns, gs, H, P), eye).reshape(ns, gs * P, gs * H)

    b_slab = jnp.concatenate([in_slab(bb_re), in_slab(bb_im)], axis=-1).astype(BF16)
    c_slab = jnp.concatenate([out_slab(c_re), -out_slab(c_im)], axis=1).astype(BF16)

    def power(k):
        mag = jnp.exp((lam_re * dt)[None] * k[:, None, None])
        ang = (lam_im * dt)[None] * k[:, None, None]
        lay = lambda v: v.reshape(k.shape[0], ns, gs * P).transpose(1, 0, 2)
        return lay(mag * jnp.cos(ang)), lay(mag * jnp.sin(ang))

    t = jnp.arange(L, dtype=F32)
    m = float(L // 2)
    em_re, em_im = power(m - t)
    ep_re, ep_im = power(t - m)
    vf_re, vf_im = power(jnp.full((1,), m + 1.0, F32))
    return b_slab, c_slab, em_re, em_im, ep_re, ep_im, vf_re, vf_im


def _s5_kernel(u_ref, b_ref, c_ref, emr_ref, emi_ref, epr_ref, epi_ref, vfr_ref, vfi_ref,
               d_ref, wglu_ref, bglu_ref, o_ref, carry_scr, s_scr, *, L):
    tc = u_ref.shape[0]
    ns = b_ref.shape[0]
    half = b_ref.shape[2] // 2

    @pl.when(pl.program_id(1) == 0)
    def _():
        carry_scr[...] = jnp.zeros_like(carry_scr)

    row = lax.broadcasted_iota(jnp.int32, (L, L), 0)
    col = lax.broadcasted_iota(jnp.int32, (L, L), 1)
    tri = (col <= row).astype(BF16)
    nc = tc // L
    bus = [_dot(u_ref[:, j * LANES:(j + 1) * LANES], b_ref[j]) for j in range(ns)]
    pres = []
    for j in range(ns):
        emr, emi = emr_ref[j], emi_ref[j]
        for c in range(nc):
            bur = bus[j][c * L:(c + 1) * L, :half]
            bui = bus[j][c * L:(c + 1) * L, half:]
            scaled = jnp.concatenate([bur * emr - bui * emi, bur * emi + bui * emr], axis=1)
            pres.append(_dot(tri, scaled.astype(BF16)))
    for j in range(ns):
        epr, epi, vfr, vfi = epr_ref[j], epi_ref[j], vfr_ref[j], vfi_ref[j]
        c_re = carry_scr[2 * j:2 * j + 1, :]
        c_im = carry_scr[2 * j + 1:2 * j + 2, :]
        for c in range(nc):
            pre = pres[j * nc + c]
            v_re = vfr * c_re - vfi * c_im
            v_im = vfr * c_im + vfi * c_re
            l_re = pre[L - 1:L, :half] + v_re
            l_im = pre[L - 1:L, half:] + v_im
            c_re = l_re * epr[L - 1:L, :] - l_im * epi[L - 1:L, :]
            c_im = l_re * epi[L - 1:L, :] + l_im * epr[L - 1:L, :]
            pr = pre[:, :half] + v_re
            pi = pre[:, half:] + v_im
            s_scr[j, c * L:(c + 1) * L, :] = jnp.concatenate(
                [pr * epr - pi * epi, pr * epi + pi * epr], axis=1).astype(BF16)
        carry_scr[2 * j:2 * j + 1, :] = c_re
        carry_scr[2 * j + 1:2 * j + 2, :] = c_im
    ys = [_dot(s_scr[j], c_ref[j]) for j in range(ns)]
    y = jnp.concatenate(ys, axis=1) + d_ref[...] * u_ref[...].astype(F32)
    ya = _gelu(y)
    o_ref[...] = (ya * _sigmoid(_dot(ya.astype(BF16), wglu_ref[...]) + bglu_ref[...])).astype(BF16)


def _s5(main, consts, d, w_glu, b_glu, tc, L):
    B, T, _ = main.shape
    W = MIX_WIDTH
    b_slab, c_slab = consts[0], consts[1]
    ns, _, two_half = b_slab.shape
    return pl.pallas_call(
        functools.partial(_s5_kernel, L=L),
        grid=(B, T // tc),
        in_specs=[pl.BlockSpec((None, tc, W), lambda b, i: (b, i, 0))]
        + [_resident(c.shape) for c in consts]
        + [_resident((1, W)), _resident((W, W)), _resident((1, W))],
        out_specs=pl.BlockSpec((None, tc, W), lambda b, i: (b, i, 0)),
        out_shape=jax.ShapeDtypeStruct((B, T, W), BF16),
        scratch_shapes=[pltpu.VMEM((2 * ns, two_half // 2), F32),
                        pltpu.VMEM((ns, tc, two_half), BF16)],
        compiler_params=_cparams("parallel", "arbitrary"),
        name="s5_mixer",
    )(main, *consts, d.reshape(1, W), w_glu, b_glu.reshape(1, W))


def _attn_kernel(q_ref, kt_ref, v_ref, o_ref, qs_scr, acc_scr, run_scr, *, tq, heads, dh):
    i = pl.program_id(1)
    row = lax.broadcasted_iota(jnp.int32, (tq, tq + LANES), 0)
    col = lax.broadcasted_iota(jnp.int32, (tq, tq + LANES), 1)
    suffix = jnp.logical_or(row > col, col >= tq).astype(BF16)
    causal = (lax.broadcasted_iota(jnp.int32, (tq, tq), 1)
              < lax.broadcasted_iota(jnp.int32, (tq, tq), 0))
    qs_scr[...] = (q_ref[...].astype(F32) * (dh ** -0.5 * LOG2E)).astype(BF16)

    hsl = [slice(h * dh, (h + 1) * dh) for h in range(heads)]

    def tile_pass(kb, diag):
        r0 = pl.multiple_of(kb * tq, tq)
        zs = [_dot(qs_scr[:, hs], kt_ref[kb, hs, :]) for hs in hsl]
        lbs, l1s = [], []
        for z in zs:
            sp = jnp.log(1.0 + jnp.exp2(-jnp.abs(z))) * LOG2E
            lb = jnp.minimum(z, 0.0) - sp
            l1 = lb - z
            lbs.append(lb)
            l1s.append(jnp.where(causal, l1, 0.0) if diag else l1)
        sums = [_dot(l1.astype(BF16), suffix) for l1 in l1s]
        wgts = []
        for h in range(heads):
            if diag:
                wgts.append(jnp.where(causal, jnp.exp2(lbs[h] + sums[h][:, :tq]), 0.0))
            else:
                run_wide = jnp.concatenate([run_scr[h]] * (tq // LANES), axis=1)
                wgts.append(jnp.exp2(lbs[h] + sums[h][:, :tq] + run_wide))
        for h, hs in enumerate(hsl):
            pv = _dot(wgts[h].astype(BF16), v_ref[pl.ds(r0, tq), hs])
            if diag:
                acc_scr[:, hs] = pv
                run_scr[h] = sums[h][:, tq:]
            else:
                acc_scr[:, hs] += pv
                run_scr[h] += sums[h][:, tq:]

    tile_pass(i, True)

    def top_run():
        top = run_scr[0]
        for h in range(1, heads):
            top = jnp.maximum(top, run_scr[h])
        return jnp.max(top)

    def cond(carry):
        j, top = carry
        return jnp.logical_and(j < i, top > POW2_ZERO_BELOW)

    def body(carry):
        j, _ = carry
        tile_pass(i - 1 - j, False)
        return j + 1, top_run()

    lax.while_loop(cond, body, (jnp.int32(0), top_run()))
    o_ref[...] = acc_scr[...].astype(BF16)


def _attention(main, kt, tq):
    B, T, _ = main.shape
    W = MIX_WIDTH
    heads = W // SB_HEAD_DIM
    return pl.pallas_call(
        functools.partial(_attn_kernel, tq=tq, heads=heads, dh=SB_HEAD_DIM),
        grid=(B, T // tq),
        in_specs=[pl.BlockSpec((None, tq, W), lambda b, i: (b, i, 1)),
                  pl.BlockSpec((None, T // tq, W, tq), lambda b, i: (b, 0, 0, 0)),
                  pl.BlockSpec((None, T, W), lambda b, i: (b, 0, 3))],
        out_specs=pl.BlockSpec((None, tq, W), lambda b, i: (b, i, 0)),
        out_shape=jax.ShapeDtypeStruct((B, T, W), BF16),
        scratch_shapes=[pltpu.VMEM((tq, W), BF16),
                        pltpu.VMEM((tq, W), F32),
                        pltpu.VMEM((heads, tq, LANES), F32)],
        compiler_params=_cparams("parallel", "arbitrary"),
        name="sb_attention",
    )(main, kt, main)


def _lru_kernel(x_ref, y_ref, cw_ref, cb_ref, wa_ref, ba_ref, wx_ref, bx_ref, lam_ref, o_ref,
                tail_scr, h_scr, a_scr, b_scr, *, pitch):
    nb, tt, w = x_ref.shape
    ns = w // LANES
    halo = tail_scr.shape[1]

    @pl.when(pl.program_id(0) == 0)
    def _():
        tail_scr[...] = jnp.zeros_like(tail_scr)
        h_scr[...] = jnp.zeros_like(h_scr)

    x = x_ref[...].astype(F32)
    xx = jnp.concatenate([tail_scr[...], x], axis=1)
    tail_scr[...] = x[:, tt - halo:, :]
    xc = cb_ref[...]
    for j in range(CONV_WIDTH):
        off = halo - (CONV_WIDTH - 1) + j
        xc = xc + cw_ref[j:j + 1, :] * xx[:, off:off + tt, :]
    xc = xc.reshape(nb * tt, w)
    xcb = xc.astype(BF16)
    r = _sigmoid(_dot(xcb, wa_ref[...]) + ba_ref[...])
    ig = _sigmoid(_dot(xcb, wx_ref[...]) + bx_ref[...])
    lam = lam_ref[...]
    log_sig_lam = jnp.minimum(lam, 0.0) - jnp.log(1.0 + jnp.exp(-jnp.abs(lam)))
    log_a = LRU_C * r * log_sig_lam
    a = jnp.exp(log_a)
    bb = jnp.sqrt(1.0 - a * a) * (ig * xc)
    for b in range(nb):
        for s in range(ns):
            a_scr[s, b * pitch:b * pitch + tt, :] = a[b * tt:(b + 1) * tt, s * LANES:(s + 1) * LANES]
            b_scr[s, b * pitch:b * pitch + tt, :] = bb[b * tt:(b + 1) * tt, s * LANES:(s + 1) * LANES]

    def step(t, h):
        rows_t = pl.ds(t, nb, stride=pitch)
        out = []
        for s in range(ns):
            hs = a_scr[s, rows_t, :] * h[s] + b_scr[s, rows_t, :]
            b_scr[s, rows_t, :] = hs
            out.append(hs)
        return tuple(out)

    h = lax.fori_loop(0, tt, step, tuple(h_scr[s] for s in range(ns)), unroll=8)
    for s in range(ns):
        h_scr[s] = h[s]
    for b in range(nb):
        for s in range(ns):
            cols = slice(s * LANES, (s + 1) * LANES)
            hb = b_scr[s, b * pitch:b * pitch + tt, :]
            o_ref[b, :, cols] = (hb * _gelu(y_ref[b, :, cols].astype(F32))).astype(BF16)


def _block_diag(w):
    n, k, _ = w.shape
    eye = jnp.eye(n, dtype=w.dtype)
    return (eye[:, None, :, None] * w[:, :, None, :]).reshape(n * k, n * k)


def _rglru(main, conv_w, conv_b, w_a, b_a, w_x, b_x, lam, tt):
    B, T, _ = main.shape
    W = MIX_WIDTH
    halo = 8
    pitch = tt + 8
    row = lambda v: v.reshape(1, W)
    return pl.pallas_call(
        functools.partial(_lru_kernel, pitch=pitch),
        grid=(T // tt,),
        in_specs=[pl.BlockSpec((B, tt, W), lambda i: (0, i, 4)),
                  pl.BlockSpec((B, tt, W), lambda i: (0, i, 5)),
                  _resident((CONV_WIDTH, W)), _resident((1, W)),
                  _resident((W, W)), _resident((1, W)),
                  _resident((W, W)), _resident((1, W)), _resident((1, W))],
        out_specs=pl.BlockSpec((B, tt, W), lambda i: (0, i, 0)),
        out_shape=jax.ShapeDtypeStruct((B, T, W), BF16),
        scratch_shapes=[pltpu.VMEM((B, halo, W), F32),
                        pltpu.VMEM((W // LANES, B, LANES), F32),
                        pltpu.VMEM((W // LANES, B * pitch, LANES), F32),
                        pltpu.VMEM((W // LANES, B * pitch, LANES), F32)],
        compiler_params=_cparams("arbitrary"),
        name="rglru",
    )(main, main, conv_w, row(conv_b), _block_diag(w_a).astype(BF16), row(b_a),
      _block_diag(w_x).astype(BF16), row(b_x), row(lam))


def _merge_kernel(s5_ref, at_ref, lr_ref, gate_ref, x_ref, wbr_ref, gb_ref, wout_ref, o_ref):
    dm = x_ref.shape[-1]
    branches = (s5_ref[...], at_ref[...], lr_ref[...])
    merged = None
    for n in range(N_BRANCH):
        gate = _sigmoid(gate_ref[:, n * dm:(n + 1) * dm].astype(F32) + gb_ref[:, n * dm:(n + 1) * dm])
        term = gate * _dot(branches[n], wbr_ref[n])
        merged = term if merged is None else merged + term
    o_ref[...] = x_ref[...] + _dot(merged.astype(BF16), wout_ref[...])


def _merge(s5, attn, lru, gates, x, w_branch, gate_b, w_out, tm):
    B, T, D = x.shape
    W = MIX_WIDTH
    tok = lambda width: pl.BlockSpec((None, tm, width), lambda b, i: (b, i, 0))
    return pl.pallas_call(
        _merge_kernel,
        grid=(B, T // tm),
        in_specs=[tok(W), tok(W), tok(W),
                  tok(N_BRANCH * D), tok(D),
                  _resident((N_BRANCH, W, D)), _resident((1, N_BRANCH * D)), _resident((D, D))],
        out_specs=tok(D),
        out_shape=jax.ShapeDtypeStruct((B, T, D), F32),
        compiler_params=_cparams("parallel", "parallel"),
        name="merge_out",
    )(s5, attn, lru, gates, x, w_branch, gate_b.reshape(1, N_BRANCH * D), w_out)


def _ffn_kernel(x_ref, g_ref, wg_ref, wu_ref, wd_ref, o_ref, *, tf):
    x = x_ref[...]
    h = _rms(x, g_ref[...]).astype(BF16)
    acc = x
    for c0 in range(0, wg_ref.shape[1], tf):
        gate = _dot(h, wg_ref[:, c0:c0 + tf])
        up = _dot(h, wu_ref[:, c0:c0 + tf])
        act = (gate * _sigmoid(gate) * up).astype(BF16)
        acc = acc + _dot(act, wd_ref[c0:c0 + tf, :])
    o_ref[...] = acc


def _ffn(x2, g, w_gate, w_up, w_down, tm, tf):
    N, D = x2.shape
    F = w_gate.shape[1]
    return pl.pallas_call(
        functools.partial(_ffn_kernel, tf=tf),
        grid=(N // tm,),
        in_specs=[pl.BlockSpec((tm, D), lambda i: (i, 0)), _resident((1, D)),
                  _resident((D, F)), _resident((D, F)), _resident((F, D))],
        out_specs=pl.BlockSpec((tm, D), lambda i: (i, 0)),
        out_shape=jax.ShapeDtypeStruct((N, D), F32),
        compiler_params=_cparams("parallel"),
        name="ffn_swiglu",
    )(x2, g.reshape(1, D), w_gate, w_up, w_down)


def _router_kernel(x_ref, g_ref, rwh_ref, rwl_ref, rb_ref, h_ref, idx_ref, wgt_ref, *, n_exp):
    h = _rms(x_ref[...], g_ref[...])
    h_hi = h.astype(BF16)
    h_ref[...] = h_hi
    h_lo = (h - h_hi.astype(F32)).astype(BF16)
    logits = _dot(h_hi, rwh_ref[...]) + _dot(h_lo, rwh_ref[...]) + _dot(h_hi, rwl_ref[...])
    lane = lax.broadcasted_iota(jnp.int32, logits.shape, 1)
    lg = jnp.where(lane < n_exp, logits + rb_ref[...], NEG_BIG)
    m1 = jnp.max(lg, axis=1, keepdims=True)
    i1 = jnp.min(jnp.where(lg == m1, lane, LANES), axis=1, keepdims=True)
    lg2 = jnp.where(lane == i1, NEG_BIG, lg)
    m2 = jnp.max(lg2, axis=1, keepdims=True)
    i2 = jnp.min(jnp.where(lg2 == m2, lane, LANES), axis=1, keepdims=True)
    e = jnp.exp(m2 - m1)
    w1 = 1.0 / (1.0 + e)
    w2 = e / (1.0 + e)
    idx_ref[...] = jnp.where(lane == 0, i1, jnp.where(lane == 1, i2, 0))
    wgt_ref[...] = jnp.where(lane == 0, w1, jnp.where(lane == 1, w2, 0.0))


def _router(x2, g, router_w, router_b, tm):
    N, D = x2.shape
    E = router_w.shape[1]
    rw = jnp.zeros((D, LANES), F32).at[:, :E].set(router_w)
    rw_hi = rw.astype(BF16)
    rw_lo = (rw - rw_hi.astype(F32)).astype(BF16)
    rb = jnp.zeros((1, LANES), F32).at[0, :E].set(router_b)
    return pl.pallas_call(
        functools.partial(_router_kernel, n_exp=E),
        grid=(N // tm,),
        in_specs=[pl.BlockSpec((tm, D), lambda i: (i, 0)), _resident((1, D)),
                  _resident((D, LANES)), _resident((D, LANES)), _resident((1, LANES))],
        out_specs=[pl.BlockSpec((tm, D), lambda i: (i, 0)),
                   pl.BlockSpec((tm, LANES), lambda i: (i, 0)),
                   pl.BlockSpec((tm, LANES), lambda i: (i, 0))],
        out_shape=[jax.ShapeDtypeStruct((N, D), BF16),
                   jax.ShapeDtypeStruct((N, LANES), jnp.int32),
                   jax.ShapeDtypeStruct((N, LANES), F32)],
        compiler_params=_cparams("parallel"),
        name="moe_router",
    )(x2, g.reshape(1, D), rw_hi, rw_lo, rb)


def _expert_kernel(te_ref, nu_ref, xs_ref, wg_ref, wu_ref, wd_ref, o_ref, acc_ref):
    t = pl.program_id(0)
    f = pl.program_id(1)
    used = t < nu_ref[0]

    @pl.when(f == 0)
    def _():
        acc_ref[...] = jnp.zeros_like(acc_ref)

    @pl.when(used)
    def _():
        xs = xs_ref[...]
        gate = _dot(xs, wg_ref[...].astype(BF16))
        up = _dot(xs, wu_ref[...].astype(BF16))
        act = (gate * _sigmoid(gate) * up).astype(BF16)
        acc_ref[...] += _dot(act, wd_ref[...].astype(BF16))

    @pl.when(f == pl.num_programs(1) - 1)
    def _():
        o_ref[...] = acc_ref[...].astype(BF16)


def _experts(xs, tile_expert, n_used, w_gate, w_up, w_down, j, tm, tf):
    S, D = xs.shape
    F = w_gate.shape[3]
    return pl.pallas_call(
        _expert_kernel,
        grid_spec=pltpu.PrefetchScalarGridSpec(
            num_scalar_prefetch=2,
            grid=(S // tm, F // tf),
            in_specs=[pl.BlockSpec((tm, D), lambda t, f, te, nu: (t, 0)),
                      pl.BlockSpec((None, None, D, tf), lambda t, f, te, nu: (j, te[t], 0, f)),
                      pl.BlockSpec((None, None, D, tf), lambda t, f, te, nu: (j, te[t], 0, f)),
                      pl.BlockSpec((None, None, tf, D), lambda t, f, te, nu: (j, te[t], f, 0))],
            out_specs=pl.BlockSpec((tm, D), lambda t, f, te, nu: (t, 0)),
            scratch_shapes=[pltpu.VMEM((tm, D), F32)]),
        out_shape=jax.ShapeDtypeStruct((S, D), BF16),
        compiler_params=_cparams("parallel", "arbitrary"),
        name="moe_experts",
    )(tile_expert, n_used, xs, w_gate, w_up, w_down)


def _route(idx, n_exp, tm):
    N, K = idx.shape
    flat = idx.reshape(-1)
    onehot = (flat[:, None] == jnp.arange(n_exp)[None, :]).astype(jnp.int32)
    rank = jnp.cumsum(onehot, axis=0) - onehot
    counts = jnp.sum(onehot, axis=0)
    padded = ((counts + tm - 1) // tm) * tm
    ends = jnp.cumsum(padded)
    starts = ends - padded
    slot = jnp.sum((starts[None, :] + rank) * onehot, axis=1)
    S = N * K + n_exp * tm
    tile_start = jnp.arange(S // tm, dtype=jnp.int32) * tm
    last_start = jnp.maximum(ends[-1] - tm, 0)
    tile_expert = jnp.sum((jnp.minimum(tile_start, last_start)[:, None] >= ends[None, :])
                          .astype(jnp.int32), axis=1)
    tile_expert = jnp.minimum(tile_expert, n_exp - 1)
    n_used = (ends[-1] // tm).astype(jnp.int32).reshape(1)
    order = jnp.argsort(flat, stable=True).astype(jnp.int32)
    per_slot = lambda v: jnp.repeat(v[tile_expert], tm)
    within = jnp.arange(S, dtype=jnp.int32) - per_slot(starts)
    compact = jnp.clip(per_slot(jnp.cumsum(counts) - counts) + within, 0, N * K - 1)
    src = jnp.where(within < per_slot(counts), order[compact] // K, jnp.arange(S, dtype=jnp.int32) % N)
    return src, slot.reshape(N, K), tile_expert, n_used


def _combine_kernel(x_ref, y0_ref, y1_ref, wgt_ref, g_ref, o_ref, *, final_norm):
    w0 = wgt_ref[:, 0:1]
    w1 = wgt_ref[:, 1:2]
    x = x_ref[...] + w0 * y0_ref[...].astype(F32) + w1 * y1_ref[...].astype(F32)
    o_ref[...] = _rms(x, g_ref[...]) if final_norm else x


def _combine(x2, y0, y1, wgt, g, tm, final_norm):
    N, D = x2.shape
    return pl.pallas_call(
        functools.partial(_combine_kernel, final_norm=final_norm),
        grid=(N // tm,),
        in_specs=[pl.BlockSpec((tm, D), lambda i: (i, 0)),
                  pl.BlockSpec((tm, D), lambda i: (i, 0)),
                  pl.BlockSpec((tm, D), lambda i: (i, 0)),
                  pl.BlockSpec((tm, LANES), lambda i: (i, 0)),
                  _resident((1, D))],
        out_specs=pl.BlockSpec((tm, D), lambda i: (i, 0)),
        out_shape=jax.ShapeDtypeStruct((N, D), F32),
        compiler_params=_cparams("parallel"),
        name="moe_combine",
    )(x2, y0, y1, wgt, g.reshape(1, D))


def _moe(x2, g, router_w, router_b, w_gate, w_up, w_down, j, final_g, tm_tok, tm_exp, tf):
    n_exp = router_w.shape[1]
    h, idx, wgt = _router(x2, g, router_w, router_b, tm_tok)
    src, slot, tile_expert, n_used = _route(idx[:, :TOP_K], n_exp, tm_exp)
    xs = jnp.take(h, src, axis=0, mode="clip")
    ys = _experts(xs, tile_expert, n_used, w_gate, w_up, w_down, j, tm_exp, tf)
    y0 = jnp.take(ys, slot[:, 0], axis=0, mode="clip")
    y1 = jnp.take(ys, slot[:, 1], axis=0, mode="clip")
    gn = final_g if final_g is not None else jnp.ones((x2.shape[1],), F32)
    return _combine(x2, y0, y1, wgt, gn, tm_tok, final_g is not None)


def _final_norm_kernel(x_ref, g_ref, o_ref):
    o_ref[...] = _rms(x_ref[...], g_ref[...])


def _final_norm(x2, g, tm):
    N, D = x2.shape
    return pl.pallas_call(
        _final_norm_kernel,
        grid=(N // tm,),
        in_specs=[pl.BlockSpec((tm, D), lambda i: (i, 0)), _resident((1, D))],
        out_specs=pl.BlockSpec((tm, D), lambda i: (i, 0)),
        out_shape=jax.ShapeDtypeStruct((N, D), F32),
        compiler_params=_cparams("parallel"),
        name="final_norm",
    )(x2, g.reshape(1, D))


def _tile(n, pref):
    t = min(n, pref)
    while n % t:
        t //= 2
    return t


def kernel(x, mix_norm_g, w_in, gate_b, s5_lambda_re, s5_lambda_im, s5_log_dt, s5_b_re, s5_b_im, s5_c_re, s5_c_im, s5_d, s5_w_glu, s5_b_glu, conv_w, conv_b, lru_w_a, lru_b_a, lru_w_x, lru_b_x, lru_lambda, w_branch, w_out, ffn_norm_g, ffn_w_gate, ffn_w_up, ffn_w_down, router_w, router_b, moe_w_gate, moe_w_up, moe_w_down, final_norm_g):
    B, T, D = x.shape
    depth = w_in.shape[0]
    W = MIX_WIDTH
    N = B * T
    tm = _tile(T, 512)
    tq = _tile(T, 256)
    tt = _tile(T, 128)
    L = _tile(T, S5_CHUNK)
    tm_exp = _tile(N, 1024)
    x = x.astype(F32)
    w_in_bf = w_in.astype(BF16)
    for layer in range(depth):
        main, kt, gates = _norm_proj(x, mix_norm_g[layer], w_in_bf, layer, tm, tq)
        consts = _s5_consts(s5_lambda_re[layer], s5_lambda_im[layer], s5_log_dt[layer],
                            s5_b_re[layer], s5_b_im[layer], s5_c_re[layer], s5_c_im[layer], L)
        s5 = _s5(main, consts, s5_d[layer], s5_w_glu[layer].astype(BF16), s5_b_glu[layer], tm, L)
        attn = _attention(main, kt, tq)
        lru = _rglru(main, conv_w[layer], conv_b[layer], lru_w_a[layer], lru_b_a[layer],
                     lru_w_x[layer], lru_b_x[layer], lru_lambda[layer], tt)
        x = _merge(s5, attn, lru, gates, x, w_branch[layer].astype(BF16),
                   gate_b[layer], w_out[layer].astype(BF16), tm)
        x2 = x.reshape(N, D)
        last = layer == depth - 1
        j = layer // 2
        if layer % 2 == 0:
            x2 = _ffn(x2, ffn_norm_g[layer], ffn_w_gate[j].astype(BF16), ffn_w_up[j].astype(BF16),
                      ffn_w_down[j].astype(BF16), tm, ffn_w_gate.shape[2] // 2)
            if last:
                x2 = _final_norm(x2, final_norm_g, tm)
        else:
            x2 = _moe(x2, ffn_norm_g[layer], router_w[j], router_b[j], moe_w_gate, moe_w_up,
                      moe_w_down, j, final_norm_g if last else None, tm, tm_exp,
                      moe_w_gate.shape[3] // 4)
        x = x2.reshape(B, T, D)
    return x
```

```python
import functools
import math

import jax
import jax.numpy as jnp
from jax import lax
from jax.experimental import pallas as pl
from jax.experimental.pallas import tpu as pltpu

F32 = jnp.float32
BF16 = jnp.bfloat16

RMS_EPS = 1e-6
MIX_WIDTH = 384
S5_GROUP = 16
S5_STATE = 64
S5_CHUNK = 128
SB_HEAD_DIM = 64
LRU_BLOCKS = 6
CONV_WIDTH = 4
LRU_C = 8.0
N_BRANCH = 3
TOP_K = 2
LANES = 128
MXU_TILE = 256
NEG_BIG = -1e30
LOG2E = 1.4426950408889634
POW2_ZERO_BELOW = -150.0
VMEM_LIMIT = 56 * 1024 * 1024


def _cparams(*sem):
    return pltpu.CompilerParams(dimension_semantics=sem, vmem_limit_bytes=VMEM_LIMIT)


def _resident(shape):
    n = len(shape)
    return pl.BlockSpec(shape, lambda *_: (0,) * n, pipeline_mode=pl.Buffered(1))


def _rms(xf, g):
    return xf * lax.rsqrt(jnp.mean(xf * xf, axis=-1, keepdims=True) + RMS_EPS) * g


def _gelu(x):
    c = math.sqrt(2.0 / math.pi)
    return 0.5 * x * (1.0 + jnp.tanh(c * (x + 0.044715 * (x * x * x))))


def _sigmoid(x):
    return 1.0 / (1.0 + jnp.exp(-x))


def _dot(a, b):
    return jnp.dot(a, b, preferred_element_type=F32)


def _norm_proj_kernel(x_ref, g_ref, w_ref, main_ref, kt_ref, gate_ref, *, w):
    h = _rms(x_ref[...], g_ref[...]).astype(BF16)
    split = -(-3 * w // MXU_TILE) * MXU_TILE
    first = _dot(h, w_ref[:, 0:split])
    main_ref[:, 0:split] = first.astype(BF16)
    main_ref[:, split:6 * w] = _dot(h, w_ref[:, split:6 * w]).astype(BF16)
    kt = first[:, 2 * w:3 * w].T
    kb = kt_ref.shape[-1]
    for c in range(kt_ref.shape[0]):
        kt_ref[c] = kt[:, c * kb:(c + 1) * kb].astype(BF16)
    d = gate_ref.shape[-1] // N_BRANCH
    for n in range(N_BRANCH):
        c0 = 6 * w + n * d
        gate_ref[:, n * d:(n + 1) * d] = _dot(h, w_ref[:, c0:c0 + d]).astype(BF16)


def _norm_proj(x, g, w_in, layer, tm, kb):
    B, T, D = x.shape
    W = MIX_WIDTH
    nk = tm // kb
    return pl.pallas_call(
        functools.partial(_norm_proj_kernel, w=W),
        grid=(B, T // tm),
        in_specs=[pl.BlockSpec((None, tm, D), lambda b, i: (b, i, 0)),
                  _resident((1, D)),
                  pl.BlockSpec((None,) + w_in.shape[1:], lambda b, i: (layer, 0, 0),
                               pipeline_mode=pl.Buffered(1))],
        out_specs=[pl.BlockSpec((None, tm, 6 * W), lambda b, i: (b, i, 0)),
                   pl.BlockSpec((None, nk, W, kb), lambda b, i: (b, i, 0, 0)),
                   pl.BlockSpec((None, tm, N_BRANCH * D), lambda b, i: (b, i, 0))],
        out_shape=[jax.ShapeDtypeStruct((B, T, 6 * W), BF16),
                   jax.ShapeDtypeStruct((B, T // kb, W, kb), BF16),
                   jax.ShapeDtypeStruct((B, T, N_BRANCH * D), BF16)],
        compiler_params=_cparams("parallel", "parallel"),
        name="norm_proj",
    )(x, g.reshape(1, D), w_in)


def _s5_consts(lam_re, lam_im, log_dt, b_re, b_im, c_re, c_im, L):
    G, P, H = b_re.shape
    ns = (G * H) // LANES
    gs = G // ns
    dt = jnp.exp(log_dt)[:, None]
    ea = jnp.exp(lam_re * dt)
    abar_re = ea * jnp.cos(lam_im * dt)
    abar_im = ea * jnp.sin(lam_im * dt)
    den = lam_re * lam_re + lam_im * lam_im
    nr = abar_re - 1.0
    coef_re = (nr * lam_re + abar_im * lam_im) / den
    coef_im = (abar_im * lam_re - nr * lam_im) / den
    bb_re = coef_re[..., None] * b_re - coef_im[..., None] * b_im
    bb_im = coef_re[..., None] * b_im + coef_im[..., None] * b_re
    eye = jnp.eye(gs, dtype=F32)

    def in_slab(bb):
        return jnp.einsum('sgph,gk->sghkp', bb.reshape(ns, gs, P, H), eye).reshape(ns, gs * H, gs * P)

    def out_slab(c):
        return jnp.einsum('sgop,gk->sgpko', c.reshape(ns, gs, H, P), eye).reshape(ns, gs * P, gs * H)

    b_slab = jnp.concatenate([in_slab(bb_re), in_slab(bb_im)], axis=-1).astype(BF16)
    c_slab = jnp.concatenate([out_slab(c_re), -out_slab(c_im)], axis=1).astype(BF16)

    def power(k):
        mag = jnp.exp((lam_re * dt)[None] * k[:, None, None])
        ang = (lam_im * dt)[None] * k[:, None, None]
        lay = lambda v: v.reshape(k.shape[0], ns, gs * P).transpose(1, 0, 2)
        return lay(mag * jnp.cos(ang)), lay(mag * jnp.sin(ang))

    t = jnp.arange(L, dtype=F32)
    m = float(L // 2)
    em_re, em_im = power(m - t)
    ep_re, ep_im = power(t - m)
    vf_re, vf_im = power(jnp.full((1,), m + 1.0, F32))
    return b_slab, c_slab, em_re, em_im, ep_re, ep_im, vf_re, vf_im


def _s5_kernel(u_ref, b_ref, c_ref, emr_ref, emi_ref, epr_ref, epi_ref, vfr_ref, vfi_ref,
               d_ref, wglu_ref, bglu_ref, o_ref, carry_scr, s_scr, *, L):
    tc = u_ref.shape[0]
    ns = b_ref.shape[0]
    half = b_ref.shape[2] // 2

    @pl.when(pl.program_id(1) == 0)
    def _():
        carry_scr[...] = jnp.zeros_like(carry_scr)

    row = lax.broadcasted_iota(jnp.int32, (L, L), 0)
    col = lax.broadcasted_iota(jnp.int32, (L, L), 1)
    tri = (col <= row).astype(BF16)
    nc = tc // L
    bus = [_dot(u_ref[:, j * LANES:(j + 1) * LANES], b_ref[j]) for j in range(ns)]
    pres = []
    for j in range(ns):
        emr, emi = emr_ref[j], emi_ref[j]
        for c in range(nc):
            bur = bus[j][c * L:(c + 1) * L, :half]
            bui = bus[j][c * L:(c + 1) * L, half:]
            scaled = jnp.concatenate([bur * emr - bui * emi, bur * emi + bui * emr], axis=1)
            pres.append(_dot(tri, scaled.astype(BF16)))
    for j in range(ns):
        epr, epi, vfr, vfi = epr_ref[j], epi_ref[j], vfr_ref[j], vfi_ref[j]
        c_re = carry_scr[2 * j:2 * j + 1, :]
        c_im = carry_scr[2 * j + 1:2 * j + 2, :]
        for c in range(nc):
            pre = pres[j * nc + c]
            v_re = vfr * c_re - vfi * c_im
            v_im = vfr * c_im + vfi * c_re
            l_re = pre[L - 1:L, :half] + v_re
            l_im = pre[L - 1:L, half:] + v_im
            c_re = l_re * epr[L - 1:L, :] - l_im * epi[L - 1:L, :]
            c_im = l_re * epi[L - 1:L, :] + l_im * epr[L - 1:L, :]
            pr = pre[:, :half] + v_re
            pi = pre[:, half:] + v_im
            s_scr[j, c * L:(c + 1) * L, :] = jnp.concatenate(
                [pr * epr - pi * epi, pr * epi + pi * epr], axis=1).astype(BF16)
        carry_scr[2 * j:2 * j + 1, :] = c_re
        carry_scr[2 * j + 1:2 * j + 2, :] = c_im
    ys = [_dot(s_scr[j], c_ref[j]) for j in range(ns)]
    y = jnp.concatenate(ys, axis=1) + d_ref[...] * u_ref[...].astype(F32)
    ya = _gelu(y)
    o_ref[...] = (ya * _sigmoid(_dot(ya.astype(BF16), wglu_ref[...]) + bglu_ref[...])).astype(BF16)


def _s5(main, consts, d, w_glu, b_glu, tc, L):
    B, T, _ = main.shape
    W = MIX_WIDTH
    b_slab, c_slab = consts[0], consts[1]
    ns, _, two_half = b_slab.shape
    return pl.pallas_call(
        functools.partial(_s5_kernel, L=L),
        grid=(B, T // tc),
        in_specs=[pl.BlockSpec((None, tc, W), lambda b, i: (b, i, 0))]
        + [_resident(c.shape) for c in consts]
        + [_resident((1, W)), _resident((W, W)), _resident((1, W))],
        out_specs=pl.BlockSpec((None, tc, W), lambda b, i: (b, i, 0)),
        out_shape=jax.ShapeDtypeStruct((B, T, W), BF16),
        scratch_shapes=[pltpu.VMEM((2 * ns, two_half // 2), F32),
                        pltpu.VMEM((ns, tc, two_half), BF16)],
        compiler_params=_cparams("parallel", "arbitrary"),
        name="s5_mixer",
    )(main, *consts, d.reshape(1, W), w_glu, b_glu.reshape(1, W))


def _attn_kernel(q_ref, kt_ref, v_ref, o_ref, qs_scr, acc_scr, run_scr, *, tq, heads, dh):
    i = pl.program_id(1)
    row = lax.broadcasted_iota(jnp.int32, (tq, tq + LANES), 0)
    col = lax.broadcasted_iota(jnp.int32, (tq, tq + LANES), 1)
    suffix = jnp.logical_or(row > col, col >= tq).astype(BF16)
    causal = (lax.broadcasted_iota(jnp.int32, (tq, tq), 1)
              < lax.broadcasted_iota(jnp.int32, (tq, tq), 0))
    qs_scr[...] = (q_ref[...].astype(F32) * (dh ** -0.5 * LOG2E)).astype(BF16)

    hsl = [slice(h * dh, (h + 1) * dh) for h in range(heads)]

    def tile_pass(kb, diag):
        r0 = pl.multiple_of(kb * tq, tq)
        zs = [_dot(qs_scr[:, hs], kt_ref[kb, hs, :]) for hs in hsl]
        lbs, l1s = [], []
        for z in zs:
            sp = jnp.log(1.0 + jnp.exp2(-jnp.abs(z))) * LOG2E
            lb = jnp.minimum(z, 0.0) - sp
            l1 = lb - z
            lbs.append(lb)
            l1s.append(jnp.where(causal, l1, 0.0) if diag else l1)
        sums = [_dot(l1.astype(BF16), suffix) for l1 in l1s]
        wgts = []
        for h in range(heads):
            if diag:
                wgts.append(jnp.where(causal, jnp.exp2(lbs[h] + sums[h][:, :tq]), 0.0))
            else:
                run_wide = jnp.concatenate([run_scr[h]] * (tq // LANES), axis=1)
                wgts.append(jnp.exp2(lbs[h] + sums[h][:, :tq] + run_wide))
        for h, hs in enumerate(hsl):
            pv = _dot(wgts[h].astype(BF16), v_ref[pl.ds(r0, tq), hs])
            if diag:
                acc_scr[:, hs] = pv
                run_scr[h] = sums[h][:, tq:]
            else:
                acc_scr[:, hs] += pv
                run_scr[h] += sums[h][:, tq:]

    tile_pass(i, True)

    def top_run():
        top = run_scr[0]
        for h in range(1, heads):
            top = jnp.maximum(top, run_scr[h])
        return jnp.max(top)

    def cond(carry):
        j, top = carry
        return jnp.logical_and(j < i, top > POW2_ZERO_BELOW)

    def body(carry):
        j, _ = carry
        tile_pass(i - 1 - j, False)
        return j + 1, top_run()

    lax.while_loop(cond, body, (jnp.int32(0), top_run()))
    o_ref[...] = acc_scr[...].astype(BF16)


def _attention(main, kt, tq):
    B, T, _ = main.shape
    W = MIX_WIDTH
    heads = W // SB_HEAD_DIM
    return pl.pallas_call(
        functools.partial(_attn_kernel, tq=tq, heads=heads, dh=SB_HEAD_DIM),
        grid=(B, T // tq),
        in_specs=[pl.BlockSpec((None, tq, W), lambda b, i: (b, i, 1)),
                  pl.BlockSpec((None, T // tq, W, tq), lambda b, i: (b, 0, 0, 0)),
                  pl.BlockSpec((None, T, W), lambda b, i: (b, 0, 3))],
        out_specs=pl.BlockSpec((None, tq, W), lambda b, i: (b, i, 0)),
        out_shape=jax.ShapeDtypeStruct((B, T, W), BF16),
        scratch_shapes=[pltpu.VMEM((tq, W), BF16),
                        pltpu.VMEM((tq, W), F32),
                        pltpu.VMEM((heads, tq, LANES), F32)],
        compiler_params=_cparams("parallel", "arbitrary"),
        name="sb_attention",
    )(main, kt, main)


def _lru_kernel(x_ref, y_ref, cw_ref, cb_ref, wa_ref, ba_ref, wx_ref, bx_ref, lam_ref, o_ref,
                tail_scr, h_scr, a_scr, b_scr, *, pitch):
    nb, tt, w = x_ref.shape
    ns = w // LANES
    halo = tail_scr.shape[1]

    @pl.when(pl.program_id(0) == 0)
    def _():
        tail_scr[...] = jnp.zeros_like(tail_scr)
        h_scr[...] = jnp.zeros_like(h_scr)

    x = x_ref[...].astype(F32)
    xx = jnp.concatenate([tail_scr[...], x], axis=1)
    tail_scr[...] = x[:, tt - halo:, :]
    xc = cb_ref[...]
    for j in range(CONV_WIDTH):
        off = halo - (CONV_WIDTH - 1) + j
        xc = xc + cw_ref[j:j + 1, :] * xx[:, off:off + tt, :]
    xc = xc.reshape(nb * tt, w)
    xcb = xc.astype(BF16)
    r = _sigmoid(_dot(xcb, wa_ref[...]) + ba_ref[...])
    ig = _sigmoid(_dot(xcb, wx_ref[...]) + bx_ref[...])
    lam = lam_ref[...]
    log_sig_lam = jnp.minimum(lam, 0.0) - jnp.log(1.0 + jnp.exp(-jnp.abs(lam)))
    log_a = LRU_C * r * log_sig_lam
    a = jnp.exp(log_a)
    bb = jnp.sqrt(1.0 - a * a) * (ig * xc)
    for b in range(nb):
        for s in range(ns):
            a_scr[s, b * pitch:b * pitch + tt, :] = a[b * tt:(b + 1) * tt, s * LANES:(s + 1) * LANES]
            b_scr[s, b * pitch:b * pitch + tt, :] = bb[b * tt:(b + 1) * tt, s * LANES:(s + 1) * LANES]

    def step(t, h):
        rows_t = pl.ds(t, nb, stride=pitch)
        out = []
        for s in range(ns):
            hs = a_scr[s, rows_t, :] * h[s] + b_scr[s, rows_t, :]
            b_scr[s, rows_t, :] = hs
            out.append(hs)
        return tuple(out)

    h = lax.fori_loop(0, tt, step, tuple(h_scr[s] for s in range(ns)), unroll=8)
    for s in range(ns):
        h_scr[s] = h[s]
    for b in range(nb):
        for s in range(ns):
            cols = slice(s * LANES, (s + 1) * LANES)
            hb = b_scr[s, b * pitch:b * pitch + tt, :]
            o_ref[b, :, cols] = (hb * _gelu(y_ref[b, :, cols].astype(F32))).astype(BF16)


def _block_diag(w):
    n, k, _ = w.shape
    eye = jnp.eye(n, dtype=w.dtype)
    return (eye[:, None, :, None] * w[:, :, None, :]).reshape(n * k, n * k)


def _rglru(main, conv_w, conv_b, w_a, b_a, w_x, b_x, lam, tt):
    B, T, _ = main.shape
    W = MIX_WIDTH
    halo = 8
    pitch = tt + 8
    row = lambda v: v.reshape(1, W)
    return pl.pallas_call(
        functools.partial(_lru_kernel, pitch=pitch),
        grid=(T // tt,),
        in_specs=[pl.BlockSpec((B, tt, W), lambda i: (0, i, 4)),
                  pl.BlockSpec((B, tt, W), lambda i: (0, i, 5)),
                  _resident((CONV_WIDTH, W)), _resident((1, W)),
                  _resident((W, W)), _resident((1, W)),
                  _resident((W, W)), _resident((1, W)), _resident((1, W))],
        out_specs=pl.BlockSpec((B, tt, W), lambda i: (0, i, 0)),
        out_shape=jax.ShapeDtypeStruct((B, T, W), BF16),
        scratch_shapes=[pltpu.VMEM((B, halo, W), F32),
                        pltpu.VMEM((W // LANES, B, LANES), F32),
                        pltpu.VMEM((W // LANES, B * pitch, LANES), F32),
                        pltpu.VMEM((W // LANES, B * pitch, LANES), F32)],
        compiler_params=_cparams("arbitrary"),
        name="rglru",
    )(main, main, conv_w, row(conv_b), _block_diag(w_a).astype(BF16), row(b_a),
      _block_diag(w_x).astype(BF16), row(b_x), row(lam))


def _merge_kernel(s5_ref, at_ref, lr_ref, gate_ref, x_ref, wbr_ref, gb_ref, wout_ref, o_ref):
    dm = x_ref.shape[-1]
    branches = (s5_ref[...], at_ref[...], lr_ref[...])
    merged = None
    for n in range(N_BRANCH):
        gate = _sigmoid(gate_ref[:, n * dm:(n + 1) * dm].astype(F32) + gb_ref[:, n * dm:(n + 1) * dm])
        term = gate * _dot(branches[n], wbr_ref[n])
        merged = term if merged is None else merged + term
    o_ref[...] = x_ref[...] + _dot(merged.astype(BF16), wout_ref[...])


def _merge(s5, attn, lru, gates, x, w_branch, gate_b, w_out, tm):
    B, T, D = x.shape
    W = MIX_WIDTH
    tok = lambda width: pl.BlockSpec((None, tm, width), lambda b, i: (b, i, 0))
    return pl.pallas_call(
        _merge_kernel,
        grid=(B, T // tm),
        in_specs=[tok(W), tok(W), tok(W),
                  tok(N_BRANCH * D), tok(D),
                  _resident((N_BRANCH, W, D)), _resident((1, N_BRANCH * D)), _resident((D, D))],
        out_specs=tok(D),
        out_shape=jax.ShapeDtypeStruct((B, T, D), F32),
        compiler_params=_cparams("parallel", "parallel"),
        name="merge_out",
    )(s5, attn, lru, gates, x, w_branch, gate_b.reshape(1, N_BRANCH * D), w_out)


def _ffn_kernel(x_ref, g_ref, wg_ref, wu_ref, wd_ref, o_ref, *, tf):
    x = x_ref[...]
    h = _rms(x, g_ref[...]).astype(BF16)
    acc = x
    ff = wg_ref.shape[1]
    for c0 in range(0, ff, tf):
        c1 = min(c0 + tf, ff)
        gate = _dot(h, wg_ref[:, c0:c1])
        up = _dot(h, wu_ref[:, c0:c1])
        act = (gate * _sigmoid(gate) * up).astype(BF16)
        acc = acc + _dot(act, wd_ref[c0:c1, :])
    o_ref[...] = acc


def _ffn(x2, g, w_gate, w_up, w_down, tm, tf):
    N, D = x2.shape
    F = w_gate.shape[1]
    return pl.pallas_call(
        functools.partial(_ffn_kernel, tf=tf),
        grid=(N // tm,),
        in_specs=[pl.BlockSpec((tm, D), lambda i: (i, 0)), _resident((1, D)),
                  _resident((D, F)), _resident((D, F)), _resident((F, D))],
        out_specs=pl.BlockSpec((tm, D), lambda i: (i, 0)),
        out_shape=jax.ShapeDtypeStruct((N, D), F32),
        compiler_params=_cparams("parallel"),
        name="ffn_swiglu",
    )(x2, g.reshape(1, D), w_gate, w_up, w_down)


def _router_kernel(x_ref, g_ref, rwh_ref, rwl_ref, rb_ref, h_ref, idx_ref, wgt_ref, *, n_exp):
    h = _rms(x_ref[...], g_ref[...])
    h_hi = h.astype(BF16)
    h_ref[...] = h_hi
    h_lo = (h - h_hi.astype(F32)).astype(BF16)
    logits = _dot(h_hi, rwh_ref[...]) + _dot(h_lo, rwh_ref[...]) + _dot(h_hi, rwl_ref[...])
    lane = lax.broadcasted_iota(jnp.int32, logits.shape, 1)
    lg = jnp.where(lane < n_exp, logits + rb_ref[...], NEG_BIG)
    m1 = jnp.max(lg, axis=1, keepdims=True)
    i1 = jnp.min(jnp.where(lg == m1, lane, LANES), axis=1, keepdims=True)
    lg2 = jnp.where(lane == i1, NEG_BIG, lg)
    m2 = jnp.max(lg2, axis=1, keepdims=True)
    i2 = jnp.min(jnp.where(lg2 == m2, lane, LANES), axis=1, keepdims=True)
    e = jnp.exp(m2 - m1)
    w1 = 1.0 / (1.0 + e)
    w2 = e / (1.0 + e)
    idx_ref[...] = jnp.where(lane == 0, i1, jnp.where(lane == 1, i2, 0))
    wgt_ref[...] = jnp.where(lane == 0, w1, jnp.where(lane == 1, w2, 0.0))


def _router(x2, g, router_w, router_b, tm):
    N, D = x2.shape
    E = router_w.shape[1]
    rw = jnp.zeros((D, LANES), F32).at[:, :E].set(router_w)
    rw_hi = rw.astype(BF16)
    rw_lo = (rw - rw_hi.astype(F32)).astype(BF16)
    rb = jnp.zeros((1, LANES), F32).at[0, :E].set(router_b)
    return pl.pallas_call(
        functools.partial(_router_kernel, n_exp=E),
        grid=(N // tm,),
        in_specs=[pl.BlockSpec((tm, D), lambda i: (i, 0)), _resident((1, D)),
                  _resident((D, LANES)), _resident((D, LANES)), _resident((1, LANES))],
        out_specs=[pl.BlockSpec((tm, D), lambda i: (i, 0)),
                   pl.BlockSpec((tm, LANES), lambda i: (i, 0)),
                   pl.BlockSpec((tm, LANES), lambda i: (i, 0))],
        out_shape=[jax.ShapeDtypeStruct((N, D), BF16),
                   jax.ShapeDtypeStruct((N, LANES), jnp.int32),
                   jax.ShapeDtypeStruct((N, LANES), F32)],
        compiler_params=_cparams("parallel"),
        name="moe_router",
    )(x2, g.reshape(1, D), rw_hi, rw_lo, rb)


def _expert_kernel(te_ref, nu_ref, xs_ref, wg_ref, wu_ref, wd_ref, o_ref, acc_ref):
    t = pl.program_id(0)
    f = pl.program_id(1)
    used = t < nu_ref[0]

    @pl.when(f == 0)
    def _():
        acc_ref[...] = jnp.zeros_like(acc_ref)

    @pl.when(used)
    def _():
        xs = xs_ref[...]
        gate = _dot(xs, wg_ref[...].astype(BF16))
        up = _dot(xs, wu_ref[...].astype(BF16))
        act = (gate * _sigmoid(gate) * up).astype(BF16)
        acc_ref[...] += _dot(act, wd_ref[...].astype(BF16))

    @pl.when(f == pl.num_programs(1) - 1)
    def _():
        o_ref[...] = acc_ref[...].astype(BF16)


def _experts(xs, tile_expert, n_used, w_gate, w_up, w_down, j, tm, tf):
    S, D = xs.shape
    F = w_gate.shape[3]
    return pl.pallas_call(
        _expert_kernel,
        grid_spec=pltpu.PrefetchScalarGridSpec(
            num_scalar_prefetch=2,
            grid=(S // tm, F // tf),
            in_specs=[pl.BlockSpec((tm, D), lambda t, f, te, nu: (t, 0)),
                      pl.BlockSpec((None, None, D, tf), lambda t, f, te, nu: (j, te[t], 0, f)),
                      pl.BlockSpec((None, None, D, tf), lambda t, f, te, nu: (j, te[t], 0, f)),
                      pl.BlockSpec((None, None, tf, D), lambda t, f, te, nu: (j, te[t], f, 0))],
            out_specs=pl.BlockSpec((tm, D), lambda t, f, te, nu: (t, 0)),
            scratch_shapes=[pltpu.VMEM((tm, D), F32)]),
        out_shape=jax.ShapeDtypeStruct((S, D), BF16),
        compiler_params=_cparams("parallel", "arbitrary"),
        name="moe_experts",
    )(tile_expert, n_used, xs, w_gate, w_up, w_down)


def _route(idx, n_exp, tm):
    N, K = idx.shape
    flat = idx.reshape(-1)
    onehot = (flat[:, None] == jnp.arange(n_exp)[None, :]).astype(jnp.int32)
    rank = jnp.cumsum(onehot, axis=0) - onehot
    counts = jnp.sum(onehot, axis=0)
    padded = ((counts + tm - 1) // tm) * tm
    ends = jnp.cumsum(padded)
    starts = ends - padded
    slot = jnp.sum((starts[None, :] + rank) * onehot, axis=1)
    S = N * K + n_exp * tm
    tile_start = jnp.arange(S // tm, dtype=jnp.int32) * tm
    last_start = jnp.maximum(ends[-1] - tm, 0)
    tile_expert = jnp.sum((jnp.minimum(tile_start, last_start)[:, None] >= ends[None, :])
                          .astype(jnp.int32), axis=1)
    tile_expert = jnp.minimum(tile_expert, n_exp - 1)
    n_used = (ends[-1] // tm).astype(jnp.int32).reshape(1)
    order = jnp.argsort(flat, stable=True).astype(jnp.int32)
    per_slot = lambda v: jnp.repeat(v[tile_expert], tm)
    within = jnp.arange(S, dtype=jnp.int32) - per_slot(starts)
    compact = jnp.clip(per_slot(jnp.cumsum(counts) - counts) + within, 0, N * K - 1)
    src = jnp.where(within < per_slot(counts), order[compact] // K, jnp.arange(S, dtype=jnp.int32) % N)
    return src, slot.reshape(N, K), tile_expert, n_used


def _combine_kernel(x_ref, y0_ref, y1_ref, wgt_ref, g_ref, o_ref, *, final_norm):
    w0 = wgt_ref[:, 0:1]
    w1 = wgt_ref[:, 1:2]
    x = x_ref[...] + w0 * y0_ref[...].astype(F32) + w1 * y1_ref[...].astype(F32)
    o_ref[...] = _rms(x, g_ref[...]) if final_norm else x


def _combine(x2, y0, y1, wgt, g, tm, final_norm):
    N, D = x2.shape
    return pl.pallas_call(
        functools.partial(_combine_kernel, final_norm=final_norm),
        grid=(N // tm,),
        in_specs=[pl.BlockSpec((tm, D), lambda i: (i, 0)),
                  pl.BlockSpec((tm, D), lambda i: (i, 0)),
                  pl.BlockSpec((tm, D), lambda i: (i, 0)),
                  pl.BlockSpec((tm, LANES), lambda i: (i, 0)),
                  _resident((1, D))],
        out_specs=pl.BlockSpec((tm, D), lambda i: (i, 0)),
        out_shape=jax.ShapeDtypeStruct((N, D), F32),
        compiler_params=_cparams("parallel"),
        name="moe_combine",
    )(x2, y0, y1, wgt, g.reshape(1, D))


def _moe(x2, g, router_w, router_b, w_gate, w_up, w_down, j, final_g, tm_tok, tm_exp, tf):
    n_exp = router_w.shape[1]
    h, idx, wgt = _router(x2, g, router_w, router_b, tm_tok)
    src, slot, tile_expert, n_used = _route(idx[:, :TOP_K], n_exp, tm_exp)
    xs = jnp.take(h, src, axis=0, mode="clip")
    ys = _experts(xs, tile_expert, n_used, w_gate, w_up, w_down, j, tm_exp, tf)
    y0 = jnp.take(ys, slot[:, 0], axis=0, mode="clip")
    y1 = jnp.take(ys, slot[:, 1], axis=0, mode="clip")
    gn = final_g if final_g is not None else jnp.ones((x2.shape[1],), F32)
    return _combine(x2, y0, y1, wgt, gn, tm_tok, final_g is not None)


def _final_norm_kernel(x_ref, g_ref, o_ref):
    o_ref[...] = _rms(x_ref[...], g_ref[...])


def _final_norm(x2, g, tm):
    N, D = x2.shape
    return pl.pallas_call(
        _final_norm_kernel,
        grid=(N // tm,),
        in_specs=[pl.BlockSpec((tm, D), lambda i: (i, 0)), _resident((1, D))],
        out_specs=pl.BlockSpec((tm, D), lambda i: (i, 0)),
        out_shape=jax.ShapeDtypeStruct((N, D), F32),
        compiler_params=_cparams("parallel"),
        name="final_norm",
    )(x2, g.reshape(1, D))


def _tile(n, pref):
    t = min(n, pref)
    while n % t:
        t //= 2
    return t


def kernel(x, mix_norm_g, w_in, gate_b, s5_lambda_re, s5_lambda_im, s5_log_dt, s5_b_re, s5_b_im, s5_c_re, s5_c_im, s5_d, s5_w_glu, s5_b_glu, conv_w, conv_b, lru_w_a, lru_b_a, lru_w_x, lru_b_x, lru_lambda, w_branch, w_out, ffn_norm_g, ffn_w_gate, ffn_w_up, ffn_w_down, router_w, router_b, moe_w_gate, moe_w_up, moe_w_down, final_norm_g):
    B, T, D = x.shape
    depth = w_in.shape[0]
    W = MIX_WIDTH
    N = B * T
    tm = _tile(T, 512)
    tq = _tile(T, 256)
    tt = _tile(T, 128)
    L = _tile(T, S5_CHUNK)
    tm_exp = _tile(N, 1024)
    x = x.astype(F32)
    w_in_bf = w_in.astype(BF16)
    for layer in range(depth):
        main, kt, gates = _norm_proj(x, mix_norm_g[layer], w_in_bf, layer, tm, tq)
        consts = _s5_consts(s5_lambda_re[layer], s5_lambda_im[layer], s5_log_dt[layer],
                            s5_b_re[layer], s5_b_im[layer], s5_c_re[layer], s5_c_im[layer], L)
        s5 = _s5(main, consts, s5_d[layer], s5_w_glu[layer].astype(BF16), s5_b_glu[layer], tm, L)
        attn = _attention(main, kt, tq)
        lru = _rglru(main, conv_w[layer], conv_b[layer], lru_w_a[layer], lru_b_a[layer],
                     lru_w_x[layer], lru_b_x[layer], lru_lambda[layer], tt)
        x = _merge(s5, attn, lru, gates, x, w_branch[layer].astype(BF16),
                   gate_b[layer], w_out[layer].astype(BF16), tm)
        x2 = x.reshape(N, D)
        last = layer == depth - 1
        j = layer // 2
        if layer % 2 == 0:
            x2 = _ffn(x2, ffn_norm_g[layer], ffn_w_gate[j].astype(BF16), ffn_w_up[j].astype(BF16),
                      ffn_w_down[j].astype(BF16), tm, 3 * MXU_TILE)
            if last:
                x2 = _final_norm(x2, final_norm_g, tm)
        else:
            x2 = _moe(x2, ffn_norm_g[layer], router_w[j], router_b[j], moe_w_gate, moe_w_up,
                      moe_w_down, j, final_norm_g if last else None, tm, tm_exp,
                      _tile(moe_w_gate.shape[3], 2 * MXU_TILE))
        x = x2.reshape(B, T, D)
    return x
```

```python
import functools
import math

import jax
import jax.numpy as jnp
from jax import lax
from jax.experimental import pallas as pl
from jax.experimental.pallas import tpu as pltpu

F32 = jnp.float32
BF16 = jnp.bfloat16

RMS_EPS = 1e-6
MIX_WIDTH = 384
S5_GROUP = 16
S5_STATE = 64
S5_CHUNK = 128
SB_HEAD_DIM = 64
LRU_BLOCKS = 6
CONV_WIDTH = 4
LRU_C = 8.0
N_BRANCH = 3
TOP_K = 2
LANES = 128
SUBLANES = 8
MXU_TILE = 256
NEG_BIG = -1e30
LOG2E = 1.4426950408889634
POW2_ZERO_BELOW = -150.0
VMEM_LIMIT = 56 * 1024 * 1024


def _cparams(*sem):
    return pltpu.CompilerParams(dimension_semantics=sem, vmem_limit_bytes=VMEM_LIMIT)


def _resident(shape):
    n = len(shape)
    return pl.BlockSpec(shape, lambda *_: (0,) * n, pipeline_mode=pl.Buffered(1))


def _rms(xf, g):
    return xf * lax.rsqrt(jnp.mean(xf * xf, axis=-1, keepdims=True) + RMS_EPS) * g


def _gelu(x):
    c = math.sqrt(2.0 / math.pi)
    return 0.5 * x * (1.0 + jnp.tanh(c * (x + 0.044715 * (x * x * x))))


def _sigmoid(x):
    return 1.0 / (1.0 + jnp.exp(-x))


def _dot(a, b):
    return jnp.dot(a, b, preferred_element_type=F32)


def _norm_proj_kernel(x_ref, g_ref, w_ref, cw_ref, cb_ref, main_ref, kt_ref, gate_ref, tail_scr, *, w):
    tm = x_ref.shape[0]
    halo = tail_scr.shape[0]

    @pl.when(pl.program_id(1) == 0)
    def _():
        tail_scr[...] = jnp.zeros_like(tail_scr)

    h = _rms(x_ref[...], g_ref[...]).astype(BF16)
    split = -(-3 * w // MXU_TILE) * MXU_TILE
    first = _dot(h, w_ref[:, 0:split])
    main_ref[:, 0:split] = first.astype(BF16)
    rest = _dot(h, w_ref[:, split:6 * w])
    main_ref[:, split:4 * w] = rest[:, 0:4 * w - split].astype(BF16)
    xl = rest[:, 4 * w - split:5 * w - split]
    xx = jnp.concatenate([tail_scr[...], xl], axis=0)
    tail_scr[...] = xl[tm - halo:, :]
    xc = cb_ref[...]
    for j in range(CONV_WIDTH):
        off = halo - (CONV_WIDTH - 1) + j
        xc = xc + cw_ref[j:j + 1, :] * xx[off:off + tm, :]
    main_ref[:, 4 * w:5 * w] = xc.astype(BF16)
    main_ref[:, 5 * w:6 * w] = _gelu(rest[:, 5 * w - split:6 * w - split]).astype(BF16)
    kt = first[:, 2 * w:3 * w].T
    kb = kt_ref.shape[-1]
    for c in range(kt_ref.shape[0]):
        kt_ref[c] = kt[:, c * kb:(c + 1) * kb].astype(BF16)
    d = gate_ref.shape[-1] // N_BRANCH
    for n in range(N_BRANCH):
        c0 = 6 * w + n * d
        gate_ref[:, n * d:(n + 1) * d] = _dot(h, w_ref[:, c0:c0 + d]).astype(BF16)


def _norm_proj(x, g, w_in, layer, conv_w, conv_b, tm, kb):
    B, T, D = x.shape
    W = MIX_WIDTH
    nk = tm // kb
    halo = 8
    return pl.pallas_call(
        functools.partial(_norm_proj_kernel, w=W),
        grid=(B, T // tm),
        in_specs=[pl.BlockSpec((None, tm, D), lambda b, i: (b, i, 0)),
                  _resident((1, D)),
                  pl.BlockSpec((None,) + w_in.shape[1:], lambda b, i: (layer, 0, 0),
                               pipeline_mode=pl.Buffered(1)),
                  _resident((CONV_WIDTH, W)), _resident((1, W))],
        out_specs=[pl.BlockSpec((None, tm, 6 * W), lambda b, i: (b, i, 0)),
                   pl.BlockSpec((None, nk, W, kb), lambda b, i: (b, i, 0, 0)),
                   pl.BlockSpec((None, tm, N_BRANCH * D), lambda b, i: (b, i, 0))],
        out_shape=[jax.ShapeDtypeStruct((B, T, 6 * W), BF16),
                   jax.ShapeDtypeStruct((B, T // kb, W, kb), BF16),
                   jax.ShapeDtypeStruct((B, T, N_BRANCH * D), BF16)],
        scratch_shapes=[pltpu.VMEM((halo, W), F32)],
        compiler_params=_cparams("parallel", "arbitrary"),
        name="norm_proj",
    )(x, g.reshape(1, D), w_in, conv_w, conv_b.reshape(1, W))


def _s5_consts(lam_re, lam_im, log_dt, b_re, b_im, c_re, c_im, L):
    G, P, H = b_re.shape
    ns = (G * H) // LANES
    gs = G // ns
    dt = jnp.exp(log_dt)[:, None]
    ea = jnp.exp(lam_re * dt)
    abar_re = ea * jnp.cos(lam_im * dt)
    abar_im = ea * jnp.sin(lam_im * dt)
    den = lam_re * lam_re + lam_im * lam_im
    nr = abar_re - 1.0
    coef_re = (nr * lam_re + abar_im * lam_im) / den
    coef_im = (abar_im * lam_re - nr * lam_im) / den
    bb_re = coef_re[..., None] * b_re - coef_im[..., None] * b_im
    bb_im = coef_re[..., None] * b_im + coef_im[..., None] * b_re
    eye = jnp.eye(gs, dtype=F32)

    def in_slab(bb):
        return jnp.einsum('sgph,gk->sghkp', bb.reshape(ns, gs, P, H), eye).reshape(ns, gs * H, gs * P)

    def out_slab(c):
        return jnp.einsum('sgop,gk->sgpko', c.reshape(ns, gs, H, P), eye).reshape(ns, gs * P, gs * H)

    b_slab = jnp.concatenate([in_slab(bb_re), in_slab(bb_im)], axis=-1).astype(BF16)
    c_slab = jnp.concatenate([out_slab(c_re), -out_slab(c_im)], axis=1).astype(BF16)

    def power(k):
        mag = jnp.exp((lam_re * dt)[None] * k[:, None, None])
        ang = (lam_im * dt)[None] * k[:, None, None]
        lay = lambda v: v.reshape(k.shape[0], ns, gs * P).transpose(1, 0, 2)
        return lay(mag * jnp.cos(ang)), lay(mag * jnp.sin(ang))

    t = jnp.arange(L, dtype=F32)
    m = float(L // 2)
    em_re, em_im = power(m - t)
    ep_re, ep_im = power(t - m)
    vf_re, vf_im = power(jnp.full((1,), m + 1.0, F32))
    return b_slab, c_slab, em_re, em_im, ep_re, ep_im, vf_re, vf_im


def _s5_kernel(u_ref, b_ref, c_ref, emr_ref, emi_ref, epr_ref, epi_ref, vfr_ref, vfi_ref,
               d_ref, wglu_ref, bglu_ref, o_ref, carry_scr, s_scr, *, L):
    tc = u_ref.shape[0]
    ns = b_ref.shape[0]
    half = b_ref.shape[2] // 2

    @pl.when(pl.program_id(1) == 0)
    def _():
        carry_scr[...] = jnp.zeros_like(carry_scr)

    row = lax.broadcasted_iota(jnp.int32, (L, L), 0)
    col = lax.broadcasted_iota(jnp.int32, (L, L), 1)
    tri = (col <= row).astype(BF16)
    nc = tc // L
    bus = [_dot(u_ref[:, j * LANES:(j + 1) * LANES], b_ref[j]) for j in range(ns)]
    pres = []
    for j in range(ns):
        emr, emi = emr_ref[j], emi_ref[j]
        for c in range(nc):
            bur = bus[j][c * L:(c + 1) * L, :half]
            bui = bus[j][c * L:(c + 1) * L, half:]
            scaled = jnp.concatenate([bur * emr - bui * emi, bur * emi + bui * emr], axis=1)
            pres.append(_dot(tri, scaled.astype(BF16)))
    for j in range(ns):
        epr, epi, vfr, vfi = epr_ref[j], epi_ref[j], vfr_ref[j], vfi_ref[j]
        c_re = carry_scr[2 * j:2 * j + 1, :]
        c_im = carry_scr[2 * j + 1:2 * j + 2, :]
        for c in range(nc):
            pre = pres[j * nc + c]
            v_re = vfr * c_re - vfi * c_im
            v_im = vfr * c_im + vfi * c_re
            l_re = pre[L - 1:L, :half] + v_re
            l_im = pre[L - 1:L, half:] + v_im
            c_re = l_re * epr[L - 1:L, :] - l_im * epi[L - 1:L, :]
            c_im = l_re * epi[L - 1:L, :] + l_im * epr[L - 1:L, :]
            pr = pre[:, :half] + v_re
            pi = pre[:, half:] + v_im
            s_scr[j, c * L:(c + 1) * L, :] = jnp.concatenate(
                [pr * epr - pi * epi, pr * epi + pi * epr], axis=1).astype(BF16)
        carry_scr[2 * j:2 * j + 1, :] = c_re
        carry_scr[2 * j + 1:2 * j + 2, :] = c_im
    ys = [_dot(s_scr[j], c_ref[j]) for j in range(ns)]
    y = jnp.concatenate(ys, axis=1) + d_ref[...] * u_ref[...].astype(F32)
    ya = _gelu(y)
    o_ref[...] = (ya * _sigmoid(_dot(ya.astype(BF16), wglu_ref[...]) + bglu_ref[...])).astype(BF16)


def _s5(main, consts, d, w_glu, b_glu, tc, L):
    B, T, _ = main.shape
    W = MIX_WIDTH
    b_slab, c_slab = consts[0], consts[1]
    ns, _, two_half = b_slab.shape
    return pl.pallas_call(
        functools.partial(_s5_kernel, L=L),
        grid=(B, T // tc),
        in_specs=[pl.BlockSpec((None, tc, W), lambda b, i: (b, i, 0))]
        + [_resident(c.shape) for c in consts]
        + [_resident((1, W)), _resident((W, W)), _resident((1, W))],
        out_specs=pl.BlockSpec((None, tc, W), lambda b, i: (b, i, 0)),
        out_shape=jax.ShapeDtypeStruct((B, T, W), BF16),
        scratch_shapes=[pltpu.VMEM((2 * ns, two_half // 2), F32),
                        pltpu.VMEM((ns, tc, two_half), BF16)],
        compiler_params=_cparams("parallel", "arbitrary"),
        name="s5_mixer",
    )(main, *consts, d.reshape(1, W), w_glu, b_glu.reshape(1, W))


def _attn_kernel(q_ref, kt_ref, v_ref, o_ref, qs_scr, acc_scr, run_scr, *, tq, heads, dh):
    i = pl.program_id(1)
    row = lax.broadcasted_iota(jnp.int32, (tq, tq + LANES), 0)
    col = lax.broadcasted_iota(jnp.int32, (tq, tq + LANES), 1)
    suffix = jnp.logical_or(row > col, col >= tq).astype(BF16)
    causal = (lax.broadcasted_iota(jnp.int32, (tq, tq), 1)
              < lax.broadcasted_iota(jnp.int32, (tq, tq), 0))
    qs_scr[...] = (q_ref[...].astype(F32) * (dh ** -0.5 * LOG2E)).astype(BF16)

    hsl = [slice(h * dh, (h + 1) * dh) for h in range(heads)]

    def tile_pass(kb, diag):
        r0 = pl.multiple_of(kb * tq, tq)
        zs = [_dot(qs_scr[:, hs], kt_ref[kb, hs, :]) for hs in hsl]
        lbs, l1s = [], []
        for z in zs:
            sp = jnp.log(1.0 + jnp.exp2(-jnp.abs(z))) * LOG2E
            lb = jnp.minimum(z, 0.0) - sp
            l1 = lb - z
            lbs.append(lb)
            l1s.append(jnp.where(causal, l1, 0.0) if diag else l1)
        sums = [_dot(l1.astype(BF16), suffix) for l1 in l1s]
        wgts = []
        for h in range(heads):
            if diag:
                wgts.append(jnp.where(causal, jnp.exp2(lbs[h] + sums[h][:, :tq]), 0.0))
            else:
                run_wide = jnp.concatenate([run_scr[h]] * (tq // LANES), axis=1)
                wgts.append(jnp.exp2(lbs[h] + sums[h][:, :tq] + run_wide))
        for h, hs in enumerate(hsl):
            pv = _dot(wgts[h].astype(BF16), v_ref[pl.ds(r0, tq), hs])
            if diag:
                acc_scr[:, hs] = pv
                run_scr[h] = sums[h][:, tq:]
            else:
                acc_scr[:, hs] += pv
                run_scr[h] += sums[h][:, tq:]

    tile_pass(i, True)

    def top_run():
        top = run_scr[0]
        for h in range(1, heads):
            top = jnp.maximum(top, run_scr[h])
        return jnp.max(top)

    def cond(carry):
        j, top = carry
        return jnp.logical_and(j < i, top > POW2_ZERO_BELOW)

    def body(carry):
        j, _ = carry
        tile_pass(i - 1 - j, False)
        return j + 1, top_run()

    lax.while_loop(cond, body, (jnp.int32(0), top_run()))
    o_ref[...] = acc_scr[...].astype(BF16)


def _attention(main, kt, tq):
    B, T, _ = main.shape
    W = MIX_WIDTH
    heads = W // SB_HEAD_DIM
    return pl.pallas_call(
        functools.partial(_attn_kernel, tq=tq, heads=heads, dh=SB_HEAD_DIM),
        grid=(B, T // tq),
        in_specs=[pl.BlockSpec((None, tq, W), lambda b, i: (b, i, 1)),
                  pl.BlockSpec((None, T // tq, W, tq), lambda b, i: (b, 0, 0, 0)),
                  pl.BlockSpec((None, T, W), lambda b, i: (b, 0, 3))],
        out_specs=pl.BlockSpec((None, tq, W), lambda b, i: (b, i, 0)),
        out_shape=jax.ShapeDtypeStruct((B, T, W), BF16),
        scratch_shapes=[pltpu.VMEM((tq, W), BF16),
                        pltpu.VMEM((tq, W), F32),
                        pltpu.VMEM((heads, tq, LANES), F32)],
        compiler_params=_cparams("parallel", "arbitrary"),
        name="sb_attention",
    )(main, kt, main)


def _lru_kernel(x_ref, y_ref, wa_ref, ba_ref, wx_ref, bx_ref, lam_ref, o_ref,
                h_scr, a_scr, b_scr, *, pitch):
    nb, tt, w = x_ref.shape
    ns = w // LANES

    @pl.when(pl.program_id(0) == 0)
    def _():
        h_scr[...] = jnp.zeros_like(h_scr)

    xcb = x_ref[...].reshape(nb * tt, w)
    xc = xcb.astype(F32)
    r = _sigmoid(_dot(xcb, wa_ref[...]) + ba_ref[...])
    ig = _sigmoid(_dot(xcb, wx_ref[...]) + bx_ref[...])
    lam = lam_ref[...]
    log_sig_lam = jnp.minimum(lam, 0.0) - jnp.log(1.0 + jnp.exp(-jnp.abs(lam)))
    log_a = LRU_C * r * log_sig_lam
    a = jnp.exp(log_a)
    bb = jnp.sqrt(1.0 - a * a) * (ig * xc)
    for b in range(nb):
        for s in range(ns):
            a_scr[s, b * pitch:b * pitch + tt, :] = a[b * tt:(b + 1) * tt, s * LANES:(s + 1) * LANES]
            b_scr[s, b * pitch:b * pitch + tt, :] = bb[b * tt:(b + 1) * tt, s * LANES:(s + 1) * LANES]

    def step(t, h):
        rows_t = pl.ds(t, nb, stride=pitch)
        out = []
        for s in range(ns):
            hs = a_scr[s, rows_t, :] * h[s] + b_scr[s, rows_t, :]
            b_scr[s, rows_t, :] = hs
            out.append(hs)
        return tuple(out)

    h = lax.fori_loop(0, tt, step, tuple(h_scr[s] for s in range(ns)), unroll=8)
    for s in range(ns):
        h_scr[s] = h[s]
    for b in range(nb):
        for s in range(ns):
            cols = slice(s * LANES, (s + 1) * LANES)
            hb = b_scr[s, b * pitch:b * pitch + tt, :]
            o_ref[b, :, cols] = (hb * y_ref[b, :, cols].astype(F32)).astype(BF16)


def _block_diag(w):
    n, k, _ = w.shape
    eye = jnp.eye(n, dtype=w.dtype)
    return (eye[:, None, :, None] * w[:, :, None, :]).reshape(n * k, n * k)


def _rglru(main, w_a, b_a, w_x, b_x, lam, tt):
    B, T, _ = main.shape
    W = MIX_WIDTH
    pitch = tt + 8
    row = lambda v: v.reshape(1, W)
    return pl.pallas_call(
        functools.partial(_lru_kernel, pitch=pitch),
        grid=(T // tt,),
        in_specs=[pl.BlockSpec((B, tt, W), lambda i: (0, i, 4)),
                  pl.BlockSpec((B, tt, W), lambda i: (0, i, 5)),
                  _resident((W, W)), _resident((1, W)),
                  _resident((W, W)), _resident((1, W)), _resident((1, W))],
        out_specs=pl.BlockSpec((B, tt, W), lambda i: (0, i, 0)),
        out_shape=jax.ShapeDtypeStruct((B, T, W), BF16),
        scratch_shapes=[pltpu.VMEM((W // LANES, B, LANES), F32),
                        pltpu.VMEM((W // LANES, B * pitch, LANES), F32),
                        pltpu.VMEM((W // LANES, B * pitch, LANES), F32)],
        compiler_params=_cparams("arbitrary"),
        name="rglru",
    )(main, main, _block_diag(w_a).astype(BF16), row(b_a),
      _block_diag(w_x).astype(BF16), row(b_x), row(lam))


def _merge_kernel(s5_ref, at_ref, lr_ref, gate_ref, x_ref, wbr_ref, gb_ref, wout_ref, o_ref):
    dm = x_ref.shape[-1]
    branches = (s5_ref[...], at_ref[...], lr_ref[...])
    merged = None
    for n in range(N_BRANCH):
        gate = _sigmoid(gate_ref[:, n * dm:(n + 1) * dm].astype(F32) + gb_ref[:, n * dm:(n + 1) * dm])
        term = gate * _dot(branches[n], wbr_ref[n])
        merged = term if merged is None else merged + term
    o_ref[...] = x_ref[...] + _dot(merged.astype(BF16), wout_ref[...])


def _merge(s5, attn, lru, gates, x, w_branch, gate_b, w_out, tm):
    B, T, D = x.shape
    W = MIX_WIDTH
    tok = lambda width: pl.BlockSpec((None, tm, width), lambda b, i: (b, i, 0))
    return pl.pallas_call(
        _merge_kernel,
        grid=(B, T // tm),
        in_specs=[tok(W), tok(W), tok(W),
                  tok(N_BRANCH * D), tok(D),
                  _resident((N_BRANCH, W, D)), _resident((1, N_BRANCH * D)), _resident((D, D))],
        out_specs=tok(D),
        out_shape=jax.ShapeDtypeStruct((B, T, D), F32),
        compiler_params=_cparams("parallel", "parallel"),
        name="merge_out",
    )(s5, attn, lru, gates, x, w_branch, gate_b.reshape(1, N_BRANCH * D), w_out)


def _ffn_kernel(x_ref, g_ref, wg_ref, wu_ref, wd_ref, o_ref, *, tf):
    x = x_ref[...]
    h = _rms(x, g_ref[...]).astype(BF16)
    acc = x
    ff = wg_ref.shape[1]
    for c0 in range(0, ff, tf):
        c1 = min(c0 + tf, ff)
        gate = _dot(h, wg_ref[:, c0:c1])
        up = _dot(h, wu_ref[:, c0:c1])
        act = (gate * _sigmoid(gate) * up).astype(BF16)
        acc = acc + _dot(act, wd_ref[c0:c1, :])
    o_ref[...] = acc


def _ffn(x2, g, w_gate, w_up, w_down, tm, tf):
    N, D = x2.shape
    F = w_gate.shape[1]
    return pl.pallas_call(
        functools.partial(_ffn_kernel, tf=tf),
        grid=(N // tm,),
        in_specs=[pl.BlockSpec((tm, D), lambda i: (i, 0)), _resident((1, D)),
                  _resident((D, F)), _resident((D, F)), _resident((F, D))],
        out_specs=pl.BlockSpec((tm, D), lambda i: (i, 0)),
        out_shape=jax.ShapeDtypeStruct((N, D), F32),
        compiler_params=_cparams("parallel"),
        name="ffn_swiglu",
    )(x2, g.reshape(1, D), w_gate, w_up, w_down)


def _router_kernel(x_ref, g_ref, rwh_ref, rwl_ref, rb_ref, h_ref, idx_ref, wgt_ref, *, n_exp):
    h = _rms(x_ref[...], g_ref[...])
    h_hi = h.astype(BF16)
    h_ref[...] = h_hi
    h_lo = (h - h_hi.astype(F32)).astype(BF16)
    logits = _dot(h_hi, rwh_ref[...]) + _dot(h_lo, rwh_ref[...]) + _dot(h_hi, rwl_ref[...])
    lane = lax.broadcasted_iota(jnp.int32, logits.shape, 1)
    lg = jnp.where(lane < n_exp, logits + rb_ref[...], NEG_BIG)
    m1 = jnp.max(lg, axis=1, keepdims=True)
    i1 = jnp.min(jnp.where(lg == m1, lane, LANES), axis=1, keepdims=True)
    lg2 = jnp.where(lane == i1, NEG_BIG, lg)
    m2 = jnp.max(lg2, axis=1, keepdims=True)
    i2 = jnp.min(jnp.where(lg2 == m2, lane, LANES), axis=1, keepdims=True)
    e = jnp.exp(m2 - m1)
    w1 = 1.0 / (1.0 + e)
    w2 = e / (1.0 + e)
    idx_ref[...] = jnp.where(lane == 0, i1, jnp.where(lane == 1, i2, 0)).T[0:idx_ref.shape[0], :]
    wgt_ref[...] = jnp.where(lane == 0, w1, jnp.where(lane == 1, w2, 0.0))


def _router(x2, g, router_w, router_b, tm):
    N, D = x2.shape
    E = router_w.shape[1]
    rw = jnp.zeros((D, LANES), F32).at[:, :E].set(router_w)
    rw_hi = rw.astype(BF16)
    rw_lo = (rw - rw_hi.astype(F32)).astype(BF16)
    rb = jnp.zeros((1, LANES), F32).at[0, :E].set(router_b)
    return pl.pallas_call(
        functools.partial(_router_kernel, n_exp=E),
        grid=(N // tm,),
        in_specs=[pl.BlockSpec((tm, D), lambda i: (i, 0)), _resident((1, D)),
                  _resident((D, LANES)), _resident((D, LANES)), _resident((1, LANES))],
        out_specs=[pl.BlockSpec((tm, D), lambda i: (i, 0)),
                   pl.BlockSpec((SUBLANES, tm), lambda i: (0, i)),
                   pl.BlockSpec((tm, LANES), lambda i: (i, 0))],
        out_shape=[jax.ShapeDtypeStruct((N, D), BF16),
                   jax.ShapeDtypeStruct((SUBLANES, N), jnp.int32),
                   jax.ShapeDtypeStruct((N, LANES), F32)],
        compiler_params=_cparams("parallel"),
        name="moe_router",
    )(x2, g.reshape(1, D), rw_hi, rw_lo, rb)


def _expert_kernel(te_ref, nu_ref, xs_ref, wg_ref, wu_ref, wd_ref, o_ref, acc_ref):
    t = pl.program_id(0)
    f = pl.program_id(1)
    used = t < nu_ref[0]

    @pl.when(f == 0)
    def _():
        acc_ref[...] = jnp.zeros_like(acc_ref)

    @pl.when(used)
    def _():
        xs = xs_ref[...]
        gate = _dot(xs, wg_ref[...].astype(BF16))
        up = _dot(xs, wu_ref[...].astype(BF16))
        act = (gate * _sigmoid(gate) * up).astype(BF16)
        acc_ref[...] += _dot(act, wd_ref[...].astype(BF16))

    @pl.when(f == pl.num_programs(1) - 1)
    def _():
        o_ref[...] = acc_ref[...].astype(BF16)


def _experts(xs, tile_expert, n_used, w_gate, w_up, w_down, j, tm, tf):
    S, D = xs.shape
    F = w_gate.shape[3]
    return pl.pallas_call(
        _expert_kernel,
        grid_spec=pltpu.PrefetchScalarGridSpec(
            num_scalar_prefetch=2,
            grid=(S // tm, F // tf),
            in_specs=[pl.BlockSpec((tm, D), lambda t, f, te, nu: (t, 0)),
                      pl.BlockSpec((None, None, D, tf), lambda t, f, te, nu: (j, te[t], 0, f)),
                      pl.BlockSpec((None, None, D, tf), lambda t, f, te, nu: (j, te[t], 0, f)),
                      pl.BlockSpec((None, None, tf, D), lambda t, f, te, nu: (j, te[t], f, 0))],
            out_specs=pl.BlockSpec((tm, D), lambda t, f, te, nu: (t, 0)),
            scratch_shapes=[pltpu.VMEM((tm, D), F32)]),
        out_shape=jax.ShapeDtypeStruct((S, D), BF16),
        compiler_params=_cparams("parallel", "arbitrary"),
        name="moe_experts",
    )(tile_expert, n_used, xs, w_gate, w_up, w_down)


def _route(idx, n_exp, tm):
    K, N = idx.shape
    flat = idx.reshape(-1)
    onehot = (flat[None, :] == jnp.arange(n_exp)[:, None]).astype(jnp.int32)
    rank = jnp.cumsum(onehot, axis=1) - onehot
    counts = jnp.sum(onehot, axis=1)
    padded = ((counts + tm - 1) // tm) * tm
    ends = jnp.cumsum(padded)
    starts = ends - padded
    slot = jnp.sum((starts[:, None] + rank) * onehot, axis=0)
    S = N * K + n_exp * tm
    tile_start = jnp.arange(S // tm, dtype=jnp.int32) * tm
    last_start = jnp.maximum(ends[-1] - tm, 0)
    tile_expert = jnp.sum((jnp.minimum(tile_start, last_start)[:, None] >= ends[None, :])
                          .astype(jnp.int32), axis=1)
    tile_expert = jnp.minimum(tile_expert, n_exp - 1)
    n_used = (ends[-1] // tm).astype(jnp.int32).reshape(1)
    order = jnp.argsort(flat, stable=True).astype(jnp.int32)
    per_slot = lambda v: jnp.repeat(v[tile_expert], tm)
    within = jnp.arange(S, dtype=jnp.int32) - per_slot(starts)
    compact = jnp.clip(per_slot(jnp.cumsum(counts) - counts) + within, 0, N * K - 1)
    src = jnp.where(within < per_slot(counts), order[compact], jnp.arange(S, dtype=jnp.int32)) % N
    return src, slot.reshape(K, N), tile_expert, n_used


def _combine_kernel(x_ref, y0_ref, y1_ref, wgt_ref, g_ref, o_ref, *, final_norm):
    w0 = wgt_ref[:, 0:1]
    w1 = wgt_ref[:, 1:2]
    x = x_ref[...] + w0 * y0_ref[...].astype(F32) + w1 * y1_ref[...].astype(F32)
    o_ref[...] = _rms(x, g_ref[...]) if final_norm else x


def _combine(x2, y0, y1, wgt, g, tm, final_norm):
    N, D = x2.shape
    return pl.pallas_call(
        functools.partial(_combine_kernel, final_norm=final_norm),
        grid=(N // tm,),
        in_specs=[pl.BlockSpec((tm, D), lambda i: (i, 0)),
                  pl.BlockSpec((tm, D), lambda i: (i, 0)),
                  pl.BlockSpec((tm, D), lambda i: (i, 0)),
                  pl.BlockSpec((tm, LANES), lambda i: (i, 0)),
                  _resident((1, D))],
        out_specs=pl.BlockSpec((tm, D), lambda i: (i, 0)),
        out_shape=jax.ShapeDtypeStruct((N, D), F32),
        compiler_params=_cparams("parallel"),
        name="moe_combine",
    )(x2, y0, y1, wgt, g.reshape(1, D))


def _moe(x2, g, router_w, router_b, w_gate, w_up, w_down, j, final_g, tm_tok, tm_exp, tf):
    n_exp = router_w.shape[1]
    h, idx, wgt = _router(x2, g, router_w, router_b, tm_tok)
    src, slot, tile_expert, n_used = _route(idx[:TOP_K], n_exp, tm_exp)
    xs = jnp.take(h, src, axis=0, mode="clip")
    ys = _experts(xs, tile_expert, n_used, w_gate, w_up, w_down, j, tm_exp, tf)
    y0 = jnp.take(ys, slot[0], axis=0, mode="clip")
    y1 = jnp.take(ys, slot[1], axis=0, mode="clip")
    gn = final_g if final_g is not None else jnp.ones((x2.shape[1],), F32)
    return _combine(x2, y0, y1, wgt, gn, tm_tok, final_g is not None)


def _final_norm_kernel(x_ref, g_ref, o_ref):
    o_ref[...] = _rms(x_ref[...], g_ref[...])


def _final_norm(x2, g, tm):
    N, D = x2.shape
    return pl.pallas_call(
        _final_norm_kernel,
        grid=(N // tm,),
        in_specs=[pl.BlockSpec((tm, D), lambda i: (i, 0)), _resident((1, D))],
        out_specs=pl.BlockSpec((tm, D), lambda i: (i, 0)),
        out_shape=jax.ShapeDtypeStruct((N, D), F32),
        compiler_params=_cparams("parallel"),
        name="final_norm",
    )(x2, g.reshape(1, D))


def _tile(n, pref):
    t = min(n, pref)
    while n % t:
        t //= 2
    return t


def kernel(x, mix_norm_g, w_in, gate_b, s5_lambda_re, s5_lambda_im, s5_log_dt, s5_b_re, s5_b_im, s5_c_re, s5_c_im, s5_d, s5_w_glu, s5_b_glu, conv_w, conv_b, lru_w_a, lru_b_a, lru_w_x, lru_b_x, lru_lambda, w_branch, w_out, ffn_norm_g, ffn_w_gate, ffn_w_up, ffn_w_down, router_w, router_b, moe_w_gate, moe_w_up, moe_w_down, final_norm_g):
    B, T, D = x.shape
    depth = w_in.shape[0]
    W = MIX_WIDTH
    N = B * T
    tm = _tile(T, 512)
    tq = _tile(T, 256)
    tt = _tile(T, 128)
    L = _tile(T, S5_CHUNK)
    tm_exp = _tile(N, 1024)
    x = x.astype(F32)
    w_in_bf = w_in.astype(BF16)
    for layer in range(depth):
        main, kt, gates = _norm_proj(x, mix_norm_g[layer], w_in_bf, layer, conv_w[layer],
                                     conv_b[layer], tm, tq)
        consts = _s5_consts(s5_lambda_re[layer], s5_lambda_im[layer], s5_log_dt[layer],
                            s5_b_re[layer], s5_b_im[layer], s5_c_re[layer], s5_c_im[layer], L)
        s5 = _s5(main, consts, s5_d[layer], s5_w_glu[layer].astype(BF16), s5_b_glu[layer], tm, L)
        attn = _attention(main, kt, tq)
        lru = _rglru(main, lru_w_a[layer], lru_b_a[layer], lru_w_x[layer], lru_b_x[layer],
                     lru_lambda[layer], tt)
        x = _merge(s5, attn, lru, gates, x, w_branch[layer].astype(BF16),
                   gate_b[layer], w_out[layer].astype(BF16), tm)
        x2 = x.reshape(N, D)
        last = layer == depth - 1
        j = layer // 2
        if layer % 2 == 0:
            x2 = _ffn(x2, ffn_norm_g[layer], ffn_w_gate[j].astype(BF16), ffn_w_up[j].astype(BF16),
                      ffn_w_down[j].astype(BF16), tm, 3 * MXU_TILE)
            if last:
                x2 = _final_norm(x2, final_norm_g, tm)
        else:
            x2 = _moe(x2, ffn_norm_g[layer], router_w[j], router_b[j], moe_w_gate, moe_w_up,
                      moe_w_down, j, final_norm_g if last else None, tm, tm_exp,
                      _tile(moe_w_gate.shape[3], 2 * MXU_TILE))
        x = x2.reshape(B, T, D)
    return x
```

```python
import functools
import math

import jax
import jax.numpy as jnp
from jax import lax
from jax.experimental import pallas as pl
from jax.experimental.pallas import tpu as pltpu

F32 = jnp.float32
BF16 = jnp.bfloat16

RMS_EPS = 1e-6
MIX_WIDTH = 384
S5_GROUP = 16
S5_STATE = 64
S5_CHUNK = 128
SB_HEAD_DIM = 64
LRU_BLOCKS = 6
CONV_WIDTH = 4
LRU_C = 8.0
N_BRANCH = 3
TOP_K = 2
LANES = 128
SUBLANES = 8
MXU_TILE = 256
NEG_BIG = -1e30
LOG2E = 1.4426950408889634
POW2_ZERO_BELOW = -150.0
VMEM_LIMIT = 56 * 1024 * 1024


def _cparams(*sem):
    return pltpu.CompilerParams(dimension_semantics=sem, vmem_limit_bytes=VMEM_LIMIT)


def _resident(shape):
    n = len(shape)
    return pl.BlockSpec(shape, lambda *_: (0,) * n, pipeline_mode=pl.Buffered(1))


def _rms(xf, g):
    return xf * lax.rsqrt(jnp.mean(xf * xf, axis=-1, keepdims=True) + RMS_EPS) * g


def _gelu(x):
    c = math.sqrt(2.0 / math.pi)
    return 0.5 * x * (1.0 + jnp.tanh(c * (x + 0.044715 * (x * x * x))))


def _sigmoid(x):
    return 1.0 / (1.0 + jnp.exp(-x))


def _dot(a, b):
    return jnp.dot(a, b, preferred_element_type=F32)


def _norm_proj_kernel(x_ref, g_ref, w_ref, cw_ref, cb_ref, main_ref, kt_ref, gate_ref, tail_scr, *, w):
    tm = x_ref.shape[0]
    halo = tail_scr.shape[0]

    @pl.when(pl.program_id(1) == 0)
    def _():
        tail_scr[...] = jnp.zeros_like(tail_scr)

    h = _rms(x_ref[...], g_ref[...]).astype(BF16)
    split = -(-3 * w // MXU_TILE) * MXU_TILE
    rest = _dot(h, w_ref[:, split:6 * w])
    main_ref[:, split:4 * w] = rest[:, 0:4 * w - split].astype(BF16)
    first = _dot(h, w_ref[:, 0:split])
    main_ref[:, 0:split] = first.astype(BF16)
    xl = rest[:, 4 * w - split:5 * w - split]
    xx = jnp.concatenate([tail_scr[...], xl], axis=0)
    tail_scr[...] = xl[tm - halo:, :]
    xc = cb_ref[...]
    for j in range(CONV_WIDTH):
        off = halo - (CONV_WIDTH - 1) + j
        xc = xc + cw_ref[j:j + 1, :] * xx[off:off + tm, :]
    main_ref[:, 4 * w:5 * w] = xc.astype(BF16)
    main_ref[:, 5 * w:6 * w] = _gelu(rest[:, 5 * w - split:6 * w - split]).astype(BF16)
    kt = first[:, 2 * w:3 * w].T
    kb = kt_ref.shape[-1]
    for c in range(kt_ref.shape[0]):
        kt_ref[c] = kt[:, c * kb:(c + 1) * kb].astype(BF16)
    d = gate_ref.shape[-1] // N_BRANCH
    for n in range(N_BRANCH):
        c0 = 6 * w + n * d
        gate_ref[:, n * d:(n + 1) * d] = _dot(h, w_ref[:, c0:c0 + d]).astype(BF16)


def _norm_proj(x, g, w_in, layer, conv_w, conv_b, tm, kb):
    B, T, D = x.shape
    W = MIX_WIDTH
    nk = tm // kb
    halo = 8
    return pl.pallas_call(
        functools.partial(_norm_proj_kernel, w=W),
        grid=(B, T // tm),
        in_specs=[pl.BlockSpec((None, tm, D), lambda b, i: (b, i, 0)),
                  _resident((1, D)),
                  pl.BlockSpec((None,) + w_in.shape[1:], lambda b, i: (layer, 0, 0),
                               pipeline_mode=pl.Buffered(1)),
                  _resident((CONV_WIDTH, W)), _resident((1, W))],
        out_specs=[pl.BlockSpec((None, tm, 6 * W), lambda b, i: (b, i, 0)),
                   pl.BlockSpec((None, nk, W, kb), lambda b, i: (b, i, 0, 0)),
                   pl.BlockSpec((None, tm, N_BRANCH * D), lambda b, i: (b, i, 0))],
        out_shape=[jax.ShapeDtypeStruct((B, T, 6 * W), BF16),
                   jax.ShapeDtypeStruct((B, T // kb, W, kb), BF16),
                   jax.ShapeDtypeStruct((B, T, N_BRANCH * D), BF16)],
        scratch_shapes=[pltpu.VMEM((halo, W), F32)],
        compiler_params=_cparams("parallel", "arbitrary"),
        name="norm_proj",
    )(x, g.reshape(1, D), w_in, conv_w, conv_b.reshape(1, W))


def _s5_consts(lam_re, lam_im, log_dt, b_re, b_im, c_re, c_im, L):
    G, P, H = b_re.shape
    ns = (G * H) // LANES
    gs = G // ns
    dt = jnp.exp(log_dt)[:, None]
    ea = jnp.exp(lam_re * dt)
    abar_re = ea * jnp.cos(lam_im * dt)
    abar_im = ea * jnp.sin(lam_im * dt)
    den = lam_re * lam_re + lam_im * lam_im
    nr = abar_re - 1.0
    coef_re = (nr * lam_re + abar_im * lam_im) / den
    coef_im = (abar_im * lam_re - nr * lam_im) / den
    bb_re = coef_re[..., None] * b_re - coef_im[..., None] * b_im
    bb_im = coef_re[..., None] * b_im + coef_im[..., None] * b_re
    eye = jnp.eye(gs, dtype=F32)

    def in_slab(bb):
        return jnp.einsum('sgph,gk->sghkp', bb.reshape(ns, gs, P, H), eye).reshape(ns, gs * H, gs * P)

    def out_slab(c):
        return jnp.einsum('sgop,gk->sgpko', c.reshape(ns, gs, H, P), eye).reshape(ns, gs * P, gs * H)

    b_slab = jnp.concatenate([in_slab(bb_re), in_slab(bb_im)], axis=-1).astype(BF16)
    c_slab = jnp.concatenate([out_slab(c_re), -out_slab(c_im)], axis=1).astype(BF16)

    def power(k):
        mag = jnp.exp((lam_re * dt)[None] * k[:, None, None])
        ang = (lam_im * dt)[None] * k[:, None, None]
        lay = lambda v: v.reshape(k.shape[0], ns, gs * P).transpose(1, 0, 2)
        return lay(mag * jnp.cos(ang)), lay(mag * jnp.sin(ang))

    t = jnp.arange(L, dtype=F32)
    m = float(L // 2)
    em_re, em_im = power(m - t)
    ep_re, ep_im = power(t - m)
    vf_re, vf_im = power(jnp.full((1,), m + 1.0, F32))
    return b_slab, c_slab, em_re, em_im, ep_re, ep_im, vf_re, vf_im


def _s5_kernel(u_ref, b_ref, c_ref, emr_ref, emi_ref, epr_ref, epi_ref, vfr_ref, vfi_ref,
               d_ref, wglu_ref, bglu_ref, o_ref, carry_scr, s_scr, *, L):
    tc = u_ref.shape[0]
    ns = b_ref.shape[0]
    half = b_ref.shape[2] // 2

    @pl.when(pl.program_id(1) == 0)
    def _():
        carry_scr[...] = jnp.zeros_like(carry_scr)

    row = lax.broadcasted_iota(jnp.int32, (L, L), 0)
    col = lax.broadcasted_iota(jnp.int32, (L, L), 1)
    tri = (col <= row).astype(BF16)
    nc = tc // L
    bus = [_dot(u_ref[:, j * LANES:(j + 1) * LANES], b_ref[j]) for j in range(ns)]
    pres = []
    for j in range(ns):
        emr, emi = emr_ref[j], emi_ref[j]
        for c in range(nc):
            bur = bus[j][c * L:(c + 1) * L, :half]
            bui = bus[j][c * L:(c + 1) * L, half:]
            scaled = jnp.concatenate([bur * emr - bui * emi, bur * emi + bui * emr], axis=1)
            pres.append(_dot(tri, scaled.astype(BF16)))
    for j in range(ns):
        epr, epi, vfr, vfi = epr_ref[j], epi_ref[j], vfr_ref[j], vfi_ref[j]
        c_re = carry_scr[2 * j:2 * j + 1, :]
        c_im = carry_scr[2 * j + 1:2 * j + 2, :]
        for c in range(nc):
            pre = pres[j * nc + c]
            v_re = vfr * c_re - vfi * c_im
            v_im = vfr * c_im + vfi * c_re
            l_re = pre[L - 1:L, :half] + v_re
            l_im = pre[L - 1:L, half:] + v_im
            c_re = l_re * epr[L - 1:L, :] - l_im * epi[L - 1:L, :]
            c_im = l_re * epi[L - 1:L, :] + l_im * epr[L - 1:L, :]
            pr = pre[:, :half] + v_re
            pi = pre[:, half:] + v_im
            s_scr[j, c * L:(c + 1) * L, :] = jnp.concatenate(
                [pr * epr - pi * epi, pr * epi + pi * epr], axis=1).astype(BF16)
        carry_scr[2 * j:2 * j + 1, :] = c_re
        carry_scr[2 * j + 1:2 * j + 2, :] = c_im
    ys = [_dot(s_scr[j], c_ref[j]) for j in range(ns)]
    y = jnp.concatenate(ys, axis=1) + d_ref[...] * u_ref[...].astype(F32)
    ya = _gelu(y)
    o_ref[...] = (ya * _sigmoid(_dot(ya.astype(BF16), wglu_ref[...]) + bglu_ref[...])).astype(BF16)


def _s5(main, consts, d, w_glu, b_glu, tc, L):
    B, T, _ = main.shape
    W = MIX_WIDTH
    b_slab, c_slab = consts[0], consts[1]
    ns, _, two_half = b_slab.shape
    return pl.pallas_call(
        functools.partial(_s5_kernel, L=L),
        grid=(B, T // tc),
        in_specs=[pl.BlockSpec((None, tc, W), lambda b, i: (b, i, 0))]
        + [_resident(c.shape) for c in consts]
        + [_resident((1, W)), _resident((W, W)), _resident((1, W))],
        out_specs=pl.BlockSpec((None, tc, W), lambda b, i: (b, i, 0)),
        out_shape=jax.ShapeDtypeStruct((B, T, W), BF16),
        scratch_shapes=[pltpu.VMEM((2 * ns, two_half // 2), F32),
                        pltpu.VMEM((ns, tc, two_half), BF16)],
        compiler_params=_cparams("parallel", "arbitrary"),
        name="s5_mixer",
    )(main, *consts, d.reshape(1, W), w_glu, b_glu.reshape(1, W))


def _attn_kernel(q_ref, kt_ref, v_ref, o_ref, qs_scr, acc_scr, run_scr, *, tq, heads, dh):
    i = pl.program_id(1)
    row = lax.broadcasted_iota(jnp.int32, (tq, tq + LANES), 0)
    col = lax.broadcasted_iota(jnp.int32, (tq, tq + LANES), 1)
    suffix = jnp.logical_or(row > col, col >= tq).astype(BF16)
    causal = (lax.broadcasted_iota(jnp.int32, (tq, tq), 1)
              < lax.broadcasted_iota(jnp.int32, (tq, tq), 0))
    qs_scr[...] = (q_ref[...].astype(F32) * (dh ** -0.5 * LOG2E)).astype(BF16)

    hsl = [slice(h * dh, (h + 1) * dh) for h in range(heads)]

    def tile_pass(kb, diag):
        r0 = pl.multiple_of(kb * tq, tq)
        zs = [_dot(qs_scr[:, hs], kt_ref[kb, hs, :]) for hs in hsl]
        lbs, l1s = [], []
        for z in zs:
            sp = jnp.log(1.0 + jnp.exp2(-jnp.abs(z))) * LOG2E
            lb = jnp.minimum(z, 0.0) - sp
            l1 = lb - z
            lbs.append(lb)
            l1s.append(jnp.where(causal, l1, 0.0) if diag else l1)
        sums = [_dot(l1.astype(BF16), suffix) for l1 in l1s]
        wgts = []
        for h in range(heads):
            if diag:
                wgts.append(jnp.where(causal, jnp.exp2(lbs[h] + sums[h][:, :tq]), 0.0))
            else:
                run_wide = jnp.concatenate([run_scr[h]] * (tq // LANES), axis=1)
                wgts.append(jnp.exp2(lbs[h] + sums[h][:, :tq] + run_wide))
        for h, hs in enumerate(hsl):
            pv = _dot(wgts[h].astype(BF16), v_ref[pl.ds(r0, tq), hs])
            if diag:
                acc_scr[:, hs] = pv
                run_scr[h] = sums[h][:, tq:]
            else:
                acc_scr[:, hs] += pv
                run_scr[h] += sums[h][:, tq:]

    tile_pass(i, True)

    def top_run():
        top = run_scr[0]
        for h in range(1, heads):
            top = jnp.maximum(top, run_scr[h])
        return jnp.max(top)

    def cond(carry):
        j, top = carry
        return jnp.logical_and(j < i, top > POW2_ZERO_BELOW)

    def body(carry):
        j, _ = carry
        tile_pass(i - 1 - j, False)
        return j + 1, top_run()

    lax.while_loop(cond, body, (jnp.int32(0), top_run()))
    o_ref[...] = acc_scr[...].astype(BF16)


def _attention(main, kt, tq):
    B, T, _ = main.shape
    W = MIX_WIDTH
    heads = W // SB_HEAD_DIM
    return pl.pallas_call(
        functools.partial(_attn_kernel, tq=tq, heads=heads, dh=SB_HEAD_DIM),
        grid=(B, T // tq),
        in_specs=[pl.BlockSpec((None, tq, W), lambda b, i: (b, i, 1)),
                  pl.BlockSpec((None, T // tq, W, tq), lambda b, i: (b, 0, 0, 0)),
                  pl.BlockSpec((None, T, W), lambda b, i: (b, 0, 3))],
        out_specs=pl.BlockSpec((None, tq, W), lambda b, i: (b, i, 0)),
        out_shape=jax.ShapeDtypeStruct((B, T, W), BF16),
        scratch_shapes=[pltpu.VMEM((tq, W), BF16),
                        pltpu.VMEM((tq, W), F32),
                        pltpu.VMEM((heads, tq, LANES), F32)],
        compiler_params=_cparams("parallel", "arbitrary"),
        name="sb_attention",
    )(main, kt, main)


def _lru_kernel(x_ref, y_ref, wa_ref, ba_ref, wx_ref, bx_ref, lam_ref, o_ref,
                h_scr, a_scr, b_scr, *, pitch):
    nb, tt, w = x_ref.shape
    ns = w // LANES

    @pl.when(pl.program_id(0) == 0)
    def _():
        h_scr[...] = jnp.zeros_like(h_scr)

    xcb = x_ref[...].reshape(nb * tt, w)
    xc = xcb.astype(F32)
    r = _sigmoid(_dot(xcb, wa_ref[...]) + ba_ref[...])
    ig = _sigmoid(_dot(xcb, wx_ref[...]) + bx_ref[...])
    lam = lam_ref[...]
    log_sig_lam = jnp.minimum(lam, 0.0) - jnp.log(1.0 + jnp.exp(-jnp.abs(lam)))
    log_a = LRU_C * r * log_sig_lam
    a = jnp.exp(log_a)
    bb = jnp.sqrt(1.0 - a * a) * (ig * xc)
    for b in range(nb):
        for s in range(ns):
            a_scr[s, b * pitch:b * pitch + tt, :] = a[b * tt:(b + 1) * tt, s * LANES:(s + 1) * LANES]
            b_scr[s, b * pitch:b * pitch + tt, :] = bb[b * tt:(b + 1) * tt, s * LANES:(s + 1) * LANES]

    def step(t, h):
        rows_t = pl.ds(t, nb, stride=pitch)
        out = []
        for s in range(ns):
            hs = a_scr[s, rows_t, :] * h[s] + b_scr[s, rows_t, :]
            b_scr[s, rows_t, :] = hs
            out.append(hs)
        return tuple(out)

    h = lax.fori_loop(0, tt, step, tuple(h_scr[s] for s in range(ns)), unroll=8)
    for s in range(ns):
        h_scr[s] = h[s]
    for b in range(nb):
        for s in range(ns):
            cols = slice(s * LANES, (s + 1) * LANES)
            hb = b_scr[s, b * pitch:b * pitch + tt, :]
            o_ref[b, :, cols] = (hb * y_ref[b, :, cols].astype(F32)).astype(BF16)


def _block_diag(w):
    n, k, _ = w.shape
    eye = jnp.eye(n, dtype=w.dtype)
    return (eye[:, None, :, None] * w[:, :, None, :]).reshape(n * k, n * k)


def _rglru(main, w_a, b_a, w_x, b_x, lam, tt):
    B, T, _ = main.shape
    W = MIX_WIDTH
    pitch = tt + 8
    row = lambda v: v.reshape(1, W)
    return pl.pallas_call(
        functools.partial(_lru_kernel, pitch=pitch),
        grid=(T // tt,),
        in_specs=[pl.BlockSpec((B, tt, W), lambda i: (0, i, 4)),
                  pl.BlockSpec((B, tt, W), lambda i: (0, i, 5)),
                  _resident((W, W)), _resident((1, W)),
                  _resident((W, W)), _resident((1, W)), _resident((1, W))],
        out_specs=pl.BlockSpec((B, tt, W), lambda i: (0, i, 0)),
        out_shape=jax.ShapeDtypeStruct((B, T, W), BF16),
        scratch_shapes=[pltpu.VMEM((W // LANES, B, LANES), F32),
                        pltpu.VMEM((W // LANES, B * pitch, LANES), F32),
                        pltpu.VMEM((W // LANES, B * pitch, LANES), F32)],
        compiler_params=_cparams("arbitrary"),
        name="rglru",
    )(main, main, _block_diag(w_a).astype(BF16), row(b_a),
      _block_diag(w_x).astype(BF16), row(b_x), row(lam))


def _merge_kernel(s5_ref, at_ref, lr_ref, gate_ref, x_ref, wbr_ref, gb_ref, wout_ref, o_ref):
    dm = x_ref.shape[-1]
    branches = (s5_ref[...], at_ref[...], lr_ref[...])
    merged = None
    for n in range(N_BRANCH):
        gate = _sigmoid(gate_ref[:, n * dm:(n + 1) * dm].astype(F32) + gb_ref[:, n * dm:(n + 1) * dm])
        term = gate * _dot(branches[n], wbr_ref[n])
        merged = term if merged is None else merged + term
    o_ref[...] = x_ref[...] + _dot(merged.astype(BF16), wout_ref[...])


def _merge(s5, attn, lru, gates, x, w_branch, gate_b, w_out, tm):
    B, T, D = x.shape
    W = MIX_WIDTH
    tok = lambda width: pl.BlockSpec((None, tm, width), lambda b, i: (b, i, 0))
    return pl.pallas_call(
        _merge_kernel,
        grid=(B, T // tm),
        in_specs=[tok(W), tok(W), tok(W),
                  tok(N_BRANCH * D), tok(D),
                  _resident((N_BRANCH, W, D)), _resident((1, N_BRANCH * D)), _resident((D, D))],
        out_specs=tok(D),
        out_shape=jax.ShapeDtypeStruct((B, T, D), F32),
        compiler_params=_cparams("parallel", "parallel"),
        name="merge_out",
    )(s5, attn, lru, gates, x, w_branch, gate_b.reshape(1, N_BRANCH * D), w_out)


def _ffn_kernel(x_ref, g_ref, wg_ref, wu_ref, wd_ref, o_ref, *, tf):
    x = x_ref[...]
    h = _rms(x, g_ref[...]).astype(BF16)
    acc = x
    ff = wg_ref.shape[1]
    for c0 in range(0, ff, tf):
        c1 = min(c0 + tf, ff)
        gate = _dot(h, wg_ref[:, c0:c1])
        up = _dot(h, wu_ref[:, c0:c1])
        act = (gate * _sigmoid(gate) * up).astype(BF16)
        acc = acc + _dot(act, wd_ref[c0:c1, :])
    o_ref[...] = acc


def _ffn(x2, g, w_gate, w_up, w_down, tm, tf):
    N, D = x2.shape
    F = w_gate.shape[1]
    return pl.pallas_call(
        functools.partial(_ffn_kernel, tf=tf),
        grid=(N // tm,),
        in_specs=[pl.BlockSpec((tm, D), lambda i: (i, 0)), _resident((1, D)),
                  _resident((D, F)), _resident((D, F)), _resident((F, D))],
        out_specs=pl.BlockSpec((tm, D), lambda i: (i, 0)),
        out_shape=jax.ShapeDtypeStruct((N, D), F32),
        compiler_params=_cparams("parallel"),
        name="ffn_swiglu",
    )(x2, g.reshape(1, D), w_gate, w_up, w_down)


def _router_kernel(x_ref, g_ref, rwh_ref, rwl_ref, rb_ref, h_ref, idx_ref, wgt_ref, *, n_exp):
    h = _rms(x_ref[...], g_ref[...])
    h_hi = h.astype(BF16)
    h_ref[...] = h_hi
    h_lo = (h - h_hi.astype(F32)).astype(BF16)
    logits = _dot(h_hi, rwh_ref[...]) + _dot(h_lo, rwh_ref[...]) + _dot(h_hi, rwl_ref[...])
    lane = lax.broadcasted_iota(jnp.int32, logits.shape, 1)
    lg = jnp.where(lane < n_exp, logits + rb_ref[...], NEG_BIG)
    m1 = jnp.max(lg, axis=1, keepdims=True)
    i1 = jnp.min(jnp.where(lg == m1, lane, LANES), axis=1, keepdims=True)
    lg2 = jnp.where(lane == i1, NEG_BIG, lg)
    m2 = jnp.max(lg2, axis=1, keepdims=True)
    i2 = jnp.min(jnp.where(lg2 == m2, lane, LANES), axis=1, keepdims=True)
    e = jnp.exp(m2 - m1)
    w1 = 1.0 / (1.0 + e)
    w2 = e / (1.0 + e)
    idx_ref[...] = jnp.where(lane == 0, i1, jnp.where(lane == 1, i2, 0)).T[0:idx_ref.shape[0], :]
    wgt_ref[...] = jnp.where(lane == 0, w1, jnp.where(lane == 1, w2, 0.0))


def _router(x2, g, router_w, router_b, tm):
    N, D = x2.shape
    E = router_w.shape[1]
    rw = jnp.zeros((D, LANES), F32).at[:, :E].set(router_w)
    rw_hi = rw.astype(BF16)
    rw_lo = (rw - rw_hi.astype(F32)).astype(BF16)
    rb = jnp.zeros((1, LANES), F32).at[0, :E].set(router_b)
    return pl.pallas_call(
        functools.partial(_router_kernel, n_exp=E),
        grid=(N // tm,),
        in_specs=[pl.BlockSpec((tm, D), lambda i: (i, 0)), _resident((1, D)),
                  _resident((D, LANES)), _resident((D, LANES)), _resident((1, LANES))],
        out_specs=[pl.BlockSpec((tm, D), lambda i: (i, 0)),
                   pl.BlockSpec((SUBLANES, tm), lambda i: (0, i)),
                   pl.BlockSpec((tm, LANES), lambda i: (i, 0))],
        out_shape=[jax.ShapeDtypeStruct((N, D), BF16),
                   jax.ShapeDtypeStruct((SUBLANES, N), jnp.int32),
                   jax.ShapeDtypeStruct((N, LANES), F32)],
        compiler_params=_cparams("parallel"),
        name="moe_router",
    )(x2, g.reshape(1, D), rw_hi, rw_lo, rb)


def _expert_kernel(te_ref, nu_ref, xs_ref, wg_ref, wu_ref, wd_ref, o_ref, acc_ref):
    t = pl.program_id(0)
    f = pl.program_id(1)
    used = t < nu_ref[0]

    @pl.when(f == 0)
    def _():
        acc_ref[...] = jnp.zeros_like(acc_ref)

    @pl.when(used)
    def _():
        xs = xs_ref[...]
        gate = _dot(xs, wg_ref[...].astype(BF16))
        up = _dot(xs, wu_ref[...].astype(BF16))
        act = (gate * _sigmoid(gate) * up).astype(BF16)
        acc_ref[...] += _dot(act, wd_ref[...].astype(BF16))

    @pl.when(f == pl.num_programs(1) - 1)
    def _():
        o_ref[...] = acc_ref[...].astype(BF16)


def _experts(xs, tile_expert, n_used, w_gate, w_up, w_down, j, tm, tf):
    S, D = xs.shape
    F = w_gate.shape[3]
    return pl.pallas_call(
        _expert_kernel,
        grid_spec=pltpu.PrefetchScalarGridSpec(
            num_scalar_prefetch=2,
            grid=(S // tm, F // tf),
            in_specs=[pl.BlockSpec((tm, D), lambda t, f, te, nu: (t, 0)),
                      pl.BlockSpec((None, None, D, tf), lambda t, f, te, nu: (j, te[t], 0, f)),
                      pl.BlockSpec((None, None, D, tf), lambda t, f, te, nu: (j, te[t], 0, f)),
                      pl.BlockSpec((None, None, tf, D), lambda t, f, te, nu: (j, te[t], f, 0))],
            out_specs=pl.BlockSpec((tm, D), lambda t, f, te, nu: (t, 0)),
            scratch_shapes=[pltpu.VMEM((tm, D), F32)]),
        out_shape=jax.ShapeDtypeStruct((S, D), BF16),
        compiler_params=_cparams("parallel", "arbitrary"),
        name="moe_experts",
    )(tile_expert, n_used, xs, w_gate, w_up, w_down)


def _route(idx, n_exp, tm):
    K, N = idx.shape
    experts = jnp.arange(n_exp, dtype=jnp.int32)
    onehot = (idx[:, None, :] == experts[None, :, None]).astype(jnp.int32)
    chosen = jnp.sum(onehot, axis=0)
    before = jnp.cumsum(chosen, axis=1) - chosen
    counts = jnp.sum(chosen, axis=1)
    padded = ((counts + tm - 1) // tm) * tm
    ends = jnp.cumsum(padded)
    starts = ends - padded
    slot = jnp.sum((starts[None, :, None] + before[None]) * onehot, axis=1).reshape(-1)
    S = N * K + n_exp * tm
    tile_start = jnp.arange(S // tm, dtype=jnp.int32) * tm
    last_start = jnp.maximum(ends[-1] - tm, 0)
    tile_expert = jnp.sum((jnp.minimum(tile_start, last_start)[:, None] >= ends[None, :])
                          .astype(jnp.int32), axis=1)
    tile_expert = jnp.minimum(tile_expert, n_exp - 1)
    n_used = (ends[-1] // tm).astype(jnp.int32).reshape(1)
    token = jnp.arange(N, dtype=jnp.int32)
    order = jnp.argsort((idx * N + token[None, :]).reshape(-1)).astype(jnp.int32)
    per_slot = lambda v: jnp.repeat(v[tile_expert], tm)
    within = jnp.arange(S, dtype=jnp.int32) - per_slot(starts)
    compact = jnp.clip(per_slot(jnp.cumsum(counts) - counts) + within, 0, N * K - 1)
    src = jnp.where(within < per_slot(counts), order[compact], jnp.arange(S, dtype=jnp.int32)) % N
    return src, slot.reshape(K, N), tile_expert, n_used


def _combine_kernel(x_ref, y0_ref, y1_ref, wgt_ref, g_ref, o_ref, *, final_norm):
    w0 = wgt_ref[:, 0:1]
    w1 = wgt_ref[:, 1:2]
    x = x_ref[...] + w0 * y0_ref[...].astype(F32) + w1 * y1_ref[...].astype(F32)
    o_ref[...] = _rms(x, g_ref[...]) if final_norm else x


def _combine(x2, y, wgt, g, tm, final_norm):
    N, D = x2.shape
    nt = N // tm
    return pl.pallas_call(
        functools.partial(_combine_kernel, final_norm=final_norm),
        grid=(nt,),
        in_specs=[pl.BlockSpec((tm, D), lambda i: (i, 0)),
                  pl.BlockSpec((tm, D), lambda i: (i, 0)),
                  pl.BlockSpec((tm, D), lambda i: (i + nt, 0)),
                  pl.BlockSpec((tm, LANES), lambda i: (i, 0)),
                  _resident((1, D))],
        out_specs=pl.BlockSpec((tm, D), lambda i: (i, 0)),
        out_shape=jax.ShapeDtypeStruct((N, D), F32),
        compiler_params=_cparams("parallel"),
        name="moe_combine",
    )(x2, y, y, wgt, g.reshape(1, D))


def _moe(x2, g, router_w, router_b, w_gate, w_up, w_down, j, final_g, tm_tok, tm_exp, tf):
    n_exp = router_w.shape[1]
    h, idx, wgt = _router(x2, g, router_w, router_b, tm_tok)
    src, slot, tile_expert, n_used = _route(idx[:TOP_K], n_exp, tm_exp)
    xs = jnp.take(h, src, axis=0, mode="clip")
    ys = _experts(xs, tile_expert, n_used, w_gate, w_up, w_down, j, tm_exp, tf)
    y = jnp.take(ys, slot.reshape(-1), axis=0, mode="clip")
    gn = final_g if final_g is not None else jnp.ones((x2.shape[1],), F32)
    return _combine(x2, y, wgt, gn, tm_tok, final_g is not None)


def _final_norm_kernel(x_ref, g_ref, o_ref):
    o_ref[...] = _rms(x_ref[...], g_ref[...])


def _final_norm(x2, g, tm):
    N, D = x2.shape
    return pl.pallas_call(
        _final_norm_kernel,
        grid=(N // tm,),
        in_specs=[pl.BlockSpec((tm, D), lambda i: (i, 0)), _resident((1, D))],
        out_specs=pl.BlockSpec((tm, D), lambda i: (i, 0)),
        out_shape=jax.ShapeDtypeStruct((N, D), F32),
        compiler_params=_cparams("parallel"),
        name="final_norm",
    )(x2, g.reshape(1, D))


def _tile(n, pref):
    t = min(n, pref)
    while n % t:
        t //= 2
    return t


def kernel(x, mix_norm_g, w_in, gate_b, s5_lambda_re, s5_lambda_im, s5_log_dt, s5_b_re, s5_b_im, s5_c_re, s5_c_im, s5_d, s5_w_glu, s5_b_glu, conv_w, conv_b, lru_w_a, lru_b_a, lru_w_x, lru_b_x, lru_lambda, w_branch, w_out, ffn_norm_g, ffn_w_gate, ffn_w_up, ffn_w_down, router_w, router_b, moe_w_gate, moe_w_up, moe_w_down, final_norm_g):
    B, T, D = x.shape
    depth = w_in.shape[0]
    W = MIX_WIDTH
    N = B * T
    tm = _tile(T, 512)
    tq = _tile(T, 256)
    tt = _tile(T, 128)
    L = _tile(T, S5_CHUNK)
    tm_exp = _tile(N, 1024)
    x = x.astype(F32)
    w_in_bf = w_in.astype(BF16)
    for layer in range(depth):
        main, kt, gates = _norm_proj(x, mix_norm_g[layer], w_in_bf, layer, conv_w[layer],
                                     conv_b[layer], tm, tq)
        consts = _s5_consts(s5_lambda_re[layer], s5_lambda_im[layer], s5_log_dt[layer],
                            s5_b_re[layer], s5_b_im[layer], s5_c_re[layer], s5_c_im[layer], L)
        s5 = _s5(main, consts, s5_d[layer], s5_w_glu[layer].astype(BF16), s5_b_glu[layer], tm, L)
        attn = _attention(main, kt, tq)
        lru = _rglru(main, lru_w_a[layer], lru_b_a[layer], lru_w_x[layer], lru_b_x[layer],
                     lru_lambda[layer], tt)
        x = _merge(s5, attn, lru, gates, x, w_branch[layer].astype(BF16),
                   gate_b[layer], w_out[layer].astype(BF16), tm)
        x2 = x.reshape(N, D)
        last = layer == depth - 1
        j = layer // 2
        if layer % 2 == 0:
            x2 = _ffn(x2, ffn_norm_g[layer], ffn_w_gate[j].astype(BF16), ffn_w_up[j].astype(BF16),
                      ffn_w_down[j].astype(BF16), tm, 3 * MXU_TILE)
            if last:
                x2 = _final_norm(x2, final_norm_g, tm)
        else:
            x2 = _moe(x2, ffn_norm_g[layer], router_w[j], router_b[j], moe_w_gate, moe_w_up,
                      moe_w_down, j, final_norm_g if last else None, tm, tm_exp,
                      _tile(moe_w_gate.shape[3], 2 * MXU_TILE))
        x = x2.reshape(B, T, D)
    return x
```

```python
import functools
import math

import jax
import jax.numpy as jnp
from jax import lax
from jax.experimental import pallas as pl
from jax.experimental.pallas import tpu as pltpu

F32 = jnp.float32
BF16 = jnp.bfloat16

RMS_EPS = 1e-6
MIX_WIDTH = 384
S5_GROUP = 16
S5_STATE = 64
S5_CHUNK = 128
S5_SMALL_CHUNK = 16
S5_MAX_EXPONENT = 80.0
S5_SAFE_EXPONENT = 60.0
SB_HEAD_DIM = 64
LRU_BLOCKS = 6
CONV_WIDTH = 4
LRU_C = 8.0
N_BRANCH = 3
TOP_K = 2
LANES = 128
SUBLANES = 8
MXU_TILE = 256
NEG_BIG = -1e30
LOG2E = 1.4426950408889634
POW2_ZERO_BELOW = -150.0
VMEM_LIMIT = 56 * 1024 * 1024


def _cparams(*sem):
    return pltpu.CompilerParams(dimension_semantics=sem, vmem_limit_bytes=VMEM_LIMIT)


def _resident(shape):
    n = len(shape)
    return pl.BlockSpec(shape, lambda *_: (0,) * n, pipeline_mode=pl.Buffered(1))


def _rms(xf, g):
    return xf * lax.rsqrt(jnp.mean(xf * xf, axis=-1, keepdims=True) + RMS_EPS) * g


def _gelu(x):
    c = math.sqrt(2.0 / math.pi)
    return 0.5 * x * (1.0 + jnp.tanh(c * (x + 0.044715 * (x * x * x))))


def _sigmoid(x):
    return 1.0 / (1.0 + jnp.exp(-x))


def _dot(a, b):
    return jnp.dot(a, b, preferred_element_type=F32)


def _norm_proj_kernel(x_ref, g_ref, w_ref, cw_ref, cb_ref, main_ref, kt_ref, gate_ref, tail_scr, *, w):
    tm = x_ref.shape[0]
    halo = tail_scr.shape[0]

    @pl.when(pl.program_id(1) == 0)
    def _():
        tail_scr[...] = jnp.zeros_like(tail_scr)

    h = _rms(x_ref[...], g_ref[...]).astype(BF16)
    split = -(-3 * w // MXU_TILE) * MXU_TILE
    rest = _dot(h, w_ref[:, split:6 * w])
    main_ref[:, split:4 * w] = rest[:, 0:4 * w - split].astype(BF16)
    first = _dot(h, w_ref[:, 0:split])
    main_ref[:, 0:split] = first.astype(BF16)
    xl = rest[:, 4 * w - split:5 * w - split]
    xx = jnp.concatenate([tail_scr[...], xl], axis=0)
    tail_scr[...] = xl[tm - halo:, :]
    xc = cb_ref[...]
    for j in range(CONV_WIDTH):
        off = halo - (CONV_WIDTH - 1) + j
        xc = xc + cw_ref[j:j + 1, :] * xx[off:off + tm, :]
    main_ref[:, 4 * w:5 * w] = xc.astype(BF16)
    main_ref[:, 5 * w:6 * w] = _gelu(rest[:, 5 * w - split:6 * w - split]).astype(BF16)
    kt = first[:, 2 * w:3 * w].T
    kb = kt_ref.shape[-1]
    for c in range(kt_ref.shape[0]):
        kt_ref[c] = kt[:, c * kb:(c + 1) * kb].astype(BF16)
    d = gate_ref.shape[-1] // N_BRANCH
    for n in range(N_BRANCH):
        c0 = 6 * w + n * d
        gate_ref[:, n * d:(n + 1) * d] = _dot(h, w_ref[:, c0:c0 + d]).astype(BF16)


def _norm_proj(x, g, w_in, layer, conv_w, conv_b, tm, kb):
    B, T, D = x.shape
    W = MIX_WIDTH
    nk = tm // kb
    halo = 8
    return pl.pallas_call(
        functools.partial(_norm_proj_kernel, w=W),
        grid=(B, T // tm),
        in_specs=[pl.BlockSpec((None, tm, D), lambda b, i: (b, i, 0)),
                  _resident((1, D)),
                  pl.BlockSpec((None,) + w_in.shape[1:], lambda b, i: (layer, 0, 0),
                               pipeline_mode=pl.Buffered(1)),
                  _resident((CONV_WIDTH, W)), _resident((1, W))],
        out_specs=[pl.BlockSpec((None, tm, 6 * W), lambda b, i: (b, i, 0)),
                   pl.BlockSpec((None, nk, W, kb), lambda b, i: (b, i, 0, 0)),
                   pl.BlockSpec((None, tm, N_BRANCH * D), lambda b, i: (b, i, 0))],
        out_shape=[jax.ShapeDtypeStruct((B, T, 6 * W), BF16),
                   jax.ShapeDtypeStruct((B, T // kb, W, kb), BF16),
                   jax.ShapeDtypeStruct((B, T, N_BRANCH * D), BF16)],
        scratch_shapes=[pltpu.VMEM((halo, W), F32)],
        compiler_params=_cparams("parallel", "arbitrary"),
        name="norm_proj",
    )(x, g.reshape(1, D), w_in, conv_w, conv_b.reshape(1, W))


def _s5_consts(lam_re, lam_im, log_dt, b_re, b_im, c_re, c_im, L):
    G, P, H = b_re.shape
    ns = (G * H) // LANES
    gs = G // ns
    dt = jnp.exp(log_dt)[:, None]
    ea = jnp.exp(lam_re * dt)
    abar_re = ea * jnp.cos(lam_im * dt)
    abar_im = ea * jnp.sin(lam_im * dt)
    den = lam_re * lam_re + lam_im * lam_im
    nr = abar_re - 1.0
    coef_re = (nr * lam_re + abar_im * lam_im) / den
    coef_im = (abar_im * lam_re - nr * lam_im) / den
    bb_re = coef_re[..., None] * b_re - coef_im[..., None] * b_im
    bb_im = coef_re[..., None] * b_im + coef_im[..., None] * b_re
    eye = jnp.eye(gs, dtype=F32)

    def in_slab(bb):
        return jnp.einsum('sgph,gk->sghkp', bb.reshape(ns, gs, P, H), eye).reshape(ns, gs * H, gs * P)

    def out_slab(c):
        return jnp.einsum('sgop,gk->sgpko', c.reshape(ns, gs, H, P), eye).reshape(ns, gs * P, gs * H)

    b_slab = jnp.concatenate([in_slab(bb_re), in_slab(bb_im)], axis=-1).astype(BF16)
    c_slab = jnp.concatenate([out_slab(c_re), -out_slab(c_im)], axis=1).astype(BF16)

    limit = S5_MAX_EXPONENT / (L // 2)
    log_mag = jnp.clip(lam_re * dt, -limit, limit)

    def power(k):
        mag = jnp.exp(log_mag[None] * k[:, None, None])
        ang = (lam_im * dt)[None] * k[:, None, None]
        lay = lambda v: v.reshape(k.shape[0], ns, gs * P).transpose(1, 0, 2)
        return lay(mag * jnp.cos(ang)), lay(mag * jnp.sin(ang))

    t = jnp.arange(L, dtype=F32)
    m = float(L // 2)
    em_re, em_im = power(m - t)
    ep_re, ep_im = power(t - m)
    vf_re, vf_im = power(jnp.full((1,), m + 1.0, F32))
    return b_slab, c_slab, em_re, em_im, ep_re, ep_im, vf_re, vf_im


def _s5_kernel(u_ref, b_ref, c_ref, emr_ref, emi_ref, epr_ref, epi_ref, vfr_ref, vfi_ref,
               d_ref, wglu_ref, bglu_ref, o_ref, carry_scr, s_scr, *, L):
    tc = u_ref.shape[0]
    ns = b_ref.shape[0]
    half = b_ref.shape[2] // 2

    @pl.when(pl.program_id(1) == 0)
    def _():
        carry_scr[...] = jnp.zeros_like(carry_scr)

    row = lax.broadcasted_iota(jnp.int32, (L, L), 0)
    col = lax.broadcasted_iota(jnp.int32, (L, L), 1)
    tri = (col <= row).astype(BF16)
    nc = tc // L
    bus = [_dot(u_ref[:, j * LANES:(j + 1) * LANES], b_ref[j]) for j in range(ns)]
    pres = []
    for j in range(ns):
        emr, emi = emr_ref[j], emi_ref[j]
        for c in range(nc):
            bur = bus[j][c * L:(c + 1) * L, :half]
            bui = bus[j][c * L:(c + 1) * L, half:]
            scaled = jnp.concatenate([bur * emr - bui * emi, bur * emi + bui * emr], axis=1)
            pres.append(_dot(tri, scaled.astype(BF16)))
    for j in range(ns):
        epr, epi, vfr, vfi = epr_ref[j], epi_ref[j], vfr_ref[j], vfi_ref[j]
        c_re = carry_scr[2 * j:2 * j + 1, :]
        c_im = carry_scr[2 * j + 1:2 * j + 2, :]
        for c in range(nc):
            pre = pres[j * nc + c]
            v_re = vfr * c_re - vfi * c_im
            v_im = vfr * c_im + vfi * c_re
            l_re = pre[L - 1:L, :half] + v_re
            l_im = pre[L - 1:L, half:] + v_im
            c_re = l_re * epr[L - 1:L, :] - l_im * epi[L - 1:L, :]
            c_im = l_re * epi[L - 1:L, :] + l_im * epr[L - 1:L, :]
            pr = pre[:, :half] + v_re
            pi = pre[:, half:] + v_im
            s_scr[j, c * L:(c + 1) * L, :] = jnp.concatenate(
                [pr * epr - pi * epi, pr * epi + pi * epr], axis=1).astype(BF16)
        carry_scr[2 * j:2 * j + 1, :] = c_re
        carry_scr[2 * j + 1:2 * j + 2, :] = c_im
    ys = [_dot(s_scr[j], c_ref[j]) for j in range(ns)]
    y = jnp.concatenate(ys, axis=1) + d_ref[...] * u_ref[...].astype(F32)
    ya = _gelu(y)
    o_ref[...] = (ya * _sigmoid(_dot(ya.astype(BF16), wglu_ref[...]) + bglu_ref[...])).astype(BF16)


def _s5(main, consts, d, w_glu, b_glu, tc, L):
    B, T, _ = main.shape
    W = MIX_WIDTH
    b_slab, c_slab = consts[0], consts[1]
    ns, _, two_half = b_slab.shape
    return pl.pallas_call(
        functools.partial(_s5_kernel, L=L),
        grid=(B, T // tc),
        in_specs=[pl.BlockSpec((None, tc, W), lambda b, i: (b, i, 0))]
        + [_resident(c.shape) for c in consts]
        + [_resident((1, W)), _resident((W, W)), _resident((1, W))],
        out_specs=pl.BlockSpec((None, tc, W), lambda b, i: (b, i, 0)),
        out_shape=jax.ShapeDtypeStruct((B, T, W), BF16),
        scratch_shapes=[pltpu.VMEM((2 * ns, two_half // 2), F32),
                        pltpu.VMEM((ns, tc, two_half), BF16)],
        compiler_params=_cparams("parallel", "arbitrary"),
        name="s5_mixer",
    )(main, *consts, d.reshape(1, W), w_glu, b_glu.reshape(1, W))


def _attn_kernel(q_ref, kt_ref, v_ref, o_ref, qs_scr, acc_scr, run_scr, *, tq, heads, dh):
    i = pl.program_id(1)
    row = lax.broadcasted_iota(jnp.int32, (tq, tq + LANES), 0)
    col = lax.broadcasted_iota(jnp.int32, (tq, tq + LANES), 1)
    suffix = jnp.logical_or(row > col, col >= tq).astype(BF16)
    causal = (lax.broadcasted_iota(jnp.int32, (tq, tq), 1)
              < lax.broadcasted_iota(jnp.int32, (tq, tq), 0))
    qs_scr[...] = (q_ref[...].astype(F32) * (dh ** -0.5 * LOG2E)).astype(BF16)

    hsl = [slice(h * dh, (h + 1) * dh) for h in range(heads)]

    def tile_pass(kb, diag):
        r0 = pl.multiple_of(kb * tq, tq)
        zs = [_dot(qs_scr[:, hs], kt_ref[kb, hs, :]) for hs in hsl]
        lbs, l1s = [], []
        for z in zs:
            sp = jnp.log(1.0 + jnp.exp2(-jnp.abs(z))) * LOG2E
            lb = jnp.minimum(z, 0.0) - sp
            l1 = lb - z
            lbs.append(lb)
            l1s.append(jnp.where(causal, l1, 0.0) if diag else l1)
        sums = [_dot(l1.astype(BF16), suffix) for l1 in l1s]
        wgts = []
        for h in range(heads):
            if diag:
                wgts.append(jnp.where(causal, jnp.exp2(lbs[h] + sums[h][:, :tq]), 0.0))
            else:
                run_wide = jnp.concatenate([run_scr[h]] * (tq // LANES), axis=1)
                wgts.append(jnp.exp2(lbs[h] + sums[h][:, :tq] + run_wide))
        for h, hs in enumerate(hsl):
            pv = _dot(wgts[h].astype(BF16), v_ref[pl.ds(r0, tq), hs])
            if diag:
                acc_scr[:, hs] = pv
                run_scr[h] = sums[h][:, tq:]
            else:
                acc_scr[:, hs] += pv
                run_scr[h] += sums[h][:, tq:]

    tile_pass(i, True)

    def top_run():
        top = run_scr[0]
        for h in range(1, heads):
            top = jnp.maximum(top, run_scr[h])
        return jnp.max(top)

    def cond(carry):
        j, top = carry
        return jnp.logical_and(j < i, top > POW2_ZERO_BELOW)

    def body(carry):
        j, _ = carry
        tile_pass(i - 1 - j, False)
        return j + 1, top_run()

    lax.while_loop(cond, body, (jnp.int32(0), top_run()))
    o_ref[...] = acc_scr[...].astype(BF16)


def _attention(main, kt, tq):
    B, T, _ = main.shape
    W = MIX_WIDTH
    heads = W // SB_HEAD_DIM
    return pl.pallas_call(
        functools.partial(_attn_kernel, tq=tq, heads=heads, dh=SB_HEAD_DIM),
        grid=(B, T // tq),
        in_specs=[pl.BlockSpec((None, tq, W), lambda b, i: (b, i, 1)),
                  pl.BlockSpec((None, T // tq, W, tq), lambda b, i: (b, 0, 0, 0)),
                  pl.BlockSpec((None, T, W), lambda b, i: (b, 0, 3))],
        out_specs=pl.BlockSpec((None, tq, W), lambda b, i: (b, i, 0)),
        out_shape=jax.ShapeDtypeStruct((B, T, W), BF16),
        scratch_shapes=[pltpu.VMEM((tq, W), BF16),
                        pltpu.VMEM((tq, W), F32),
                        pltpu.VMEM((heads, tq, LANES), F32)],
        compiler_params=_cparams("parallel", "arbitrary"),
        name="sb_attention",
    )(main, kt, main)


def _lru_kernel(x_ref, y_ref, wa_ref, ba_ref, wx_ref, bx_ref, lam_ref, o_ref,
                h_scr, a_scr, b_scr, *, pitch):
    nb, tt, w = x_ref.shape
    ns = w // LANES

    @pl.when(pl.program_id(0) == 0)
    def _():
        h_scr[...] = jnp.zeros_like(h_scr)

    xcb = x_ref[...].reshape(nb * tt, w)
    xc = xcb.astype(F32)
    r = _sigmoid(_dot(xcb, wa_ref[...]) + ba_ref[...])
    ig = _sigmoid(_dot(xcb, wx_ref[...]) + bx_ref[...])
    lam = lam_ref[...]
    log_sig_lam = jnp.minimum(lam, 0.0) - jnp.log(1.0 + jnp.exp(-jnp.abs(lam)))
    log_a = LRU_C * r * log_sig_lam
    a = jnp.exp(log_a)
    bb = jnp.sqrt(1.0 - a * a) * (ig * xc)
    for b in range(nb):
        for s in range(ns):
            a_scr[s, b * pitch:b * pitch + tt, :] = a[b * tt:(b + 1) * tt, s * LANES:(s + 1) * LANES]
            b_scr[s, b * pitch:b * pitch + tt, :] = bb[b * tt:(b + 1) * tt, s * LANES:(s + 1) * LANES]

    def step(t, h):
        rows_t = pl.ds(t, nb, stride=pitch)
        out = []
        for s in range(ns):
            hs = a_scr[s, rows_t, :] * h[s] + b_scr[s, rows_t, :]
            b_scr[s, rows_t, :] = hs
            out.append(hs)
        return tuple(out)

    h = lax.fori_loop(0, tt, step, tuple(h_scr[s] for s in range(ns)), unroll=8)
    for s in range(ns):
        h_scr[s] = h[s]
    for b in range(nb):
        for s in range(ns):
            cols = slice(s * LANES, (s + 1) * LANES)
            hb = b_scr[s, b * pitch:b * pitch + tt, :]
            o_ref[b, :, cols] = (hb * y_ref[b, :, cols].astype(F32)).astype(BF16)


def _block_diag(w):
    n, k, _ = w.shape
    eye = jnp.eye(n, dtype=w.dtype)
    return (eye[:, None, :, None] * w[:, :, None, :]).reshape(n * k, n * k)


def _rglru(main, w_a, b_a, w_x, b_x, lam, tt):
    B, T, _ = main.shape
    W = MIX_WIDTH
    pitch = tt + 8
    row = lambda v: v.reshape(1, W)
    return pl.pallas_call(
        functools.partial(_lru_kernel, pitch=pitch),
        grid=(T // tt,),
        in_specs=[pl.BlockSpec((B, tt, W), lambda i: (0, i, 4)),
                  pl.BlockSpec((B, tt, W), lambda i: (0, i, 5)),
                  _resident((W, W)), _resident((1, W)),
                  _resident((W, W)), _resident((1, W)), _resident((1, W))],
        out_specs=pl.BlockSpec((B, tt, W), lambda i: (0, i, 0)),
        out_shape=jax.ShapeDtypeStruct((B, T, W), BF16),
        scratch_shapes=[pltpu.VMEM((W // LANES, B, LANES), F32),
                        pltpu.VMEM((W // LANES, B * pitch, LANES), F32),
                        pltpu.VMEM((W // LANES, B * pitch, LANES), F32)],
        compiler_params=_cparams("arbitrary"),
        name="rglru",
    )(main, main, _block_diag(w_a).astype(BF16), row(b_a),
      _block_diag(w_x).astype(BF16), row(b_x), row(lam))


def _merge_kernel(s5_ref, at_ref, lr_ref, gate_ref, x_ref, wbr_ref, gb_ref, wout_ref, o_ref):
    dm = x_ref.shape[-1]
    branches = (s5_ref[...], at_ref[...], lr_ref[...])
    merged = None
    for n in range(N_BRANCH):
        gate = _sigmoid(gate_ref[:, n * dm:(n + 1) * dm].astype(F32) + gb_ref[:, n * dm:(n + 1) * dm])
        term = gate * _dot(branches[n], wbr_ref[n])
        merged = term if merged is None else merged + term
    o_ref[...] = x_ref[...] + _dot(merged.astype(BF16), wout_ref[...])


def _merge(s5, attn, lru, gates, x, w_branch, gate_b, w_out, tm):
    B, T, D = x.shape
    W = MIX_WIDTH
    tok = lambda width: pl.BlockSpec((None, tm, width), lambda b, i: (b, i, 0))
    return pl.pallas_call(
        _merge_kernel,
        grid=(B, T // tm),
        in_specs=[tok(W), tok(W), tok(W),
                  tok(N_BRANCH * D), tok(D),
                  _resident((N_BRANCH, W, D)), _resident((1, N_BRANCH * D)), _resident((D, D))],
        out_specs=tok(D),
        out_shape=jax.ShapeDtypeStruct((B, T, D), F32),
        compiler_params=_cparams("parallel", "parallel"),
        name="merge_out",
    )(s5, attn, lru, gates, x, w_branch, gate_b.reshape(1, N_BRANCH * D), w_out)


def _ffn_kernel(x_ref, g_ref, wg_ref, wu_ref, wd_ref, o_ref, *, tf):
    x = x_ref[...]
    h = _rms(x, g_ref[...]).astype(BF16)
    acc = x
    ff = wg_ref.shape[1]
    for c0 in range(0, ff, tf):
        c1 = min(c0 + tf, ff)
        gate = _dot(h, wg_ref[:, c0:c1])
        up = _dot(h, wu_ref[:, c0:c1])
        act = (gate * _sigmoid(gate) * up).astype(BF16)
        acc = acc + _dot(act, wd_ref[c0:c1, :])
    o_ref[...] = acc


def _ffn(x2, g, w_gate, w_up, w_down, tm, tf):
    N, D = x2.shape
    F = w_gate.shape[1]
    return pl.pallas_call(
        functools.partial(_ffn_kernel, tf=tf),
        grid=(N // tm,),
        in_specs=[pl.BlockSpec((tm, D), lambda i: (i, 0)), _resident((1, D)),
                  _resident((D, F)), _resident((D, F)), _resident((F, D))],
        out_specs=pl.BlockSpec((tm, D), lambda i: (i, 0)),
        out_shape=jax.ShapeDtypeStruct((N, D), F32),
        compiler_params=_cparams("parallel"),
        name="ffn_swiglu",
    )(x2, g.reshape(1, D), w_gate, w_up, w_down)


def _router_kernel(x_ref, g_ref, rwh_ref, rwl_ref, rb_ref, h_ref, idx_ref, wgt_ref, *, n_exp):
    h = _rms(x_ref[...], g_ref[...])
    h_hi = h.astype(BF16)
    h_ref[...] = h_hi
    h_lo = (h - h_hi.astype(F32)).astype(BF16)
    logits = _dot(h_hi, rwh_ref[...]) + _dot(h_lo, rwh_ref[...]) + _dot(h_hi, rwl_ref[...])
    lane = lax.broadcasted_iota(jnp.int32, logits.shape, 1)
    lg = jnp.where(lane < n_exp, logits + rb_ref[...], NEG_BIG)
    m1 = jnp.max(lg, axis=1, keepdims=True)
    i1 = jnp.min(jnp.where(lg == m1, lane, LANES), axis=1, keepdims=True)
    lg2 = jnp.where(lane == i1, NEG_BIG, lg)
    m2 = jnp.max(lg2, axis=1, keepdims=True)
    i2 = jnp.min(jnp.where(lg2 == m2, lane, LANES), axis=1, keepdims=True)
    e = jnp.exp(m2 - m1)
    w1 = 1.0 / (1.0 + e)
    w2 = e / (1.0 + e)
    idx_ref[...] = jnp.where(lane == 0, i1, jnp.where(lane == 1, i2, 0)).T[0:idx_ref.shape[0], :]
    wgt_ref[...] = jnp.where(lane == 0, w1, jnp.where(lane == 1, w2, 0.0))


def _router(x2, g, router_w, router_b, tm):
    N, D = x2.shape
    E = router_w.shape[1]
    rw = jnp.zeros((D, LANES), F32).at[:, :E].set(router_w)
    rw_hi = rw.astype(BF16)
    rw_lo = (rw - rw_hi.astype(F32)).astype(BF16)
    rb = jnp.zeros((1, LANES), F32).at[0, :E].set(router_b)
    return pl.pallas_call(
        functools.partial(_router_kernel, n_exp=E),
        grid=(N // tm,),
        in_specs=[pl.BlockSpec((tm, D), lambda i: (i, 0)), _resident((1, D)),
                  _resident((D, LANES)), _resident((D, LANES)), _resident((1, LANES))],
        out_specs=[pl.BlockSpec((tm, D), lambda i: (i, 0)),
                   pl.BlockSpec((SUBLANES, tm), lambda i: (0, i)),
                   pl.BlockSpec((tm, LANES), lambda i: (i, 0))],
        out_shape=[jax.ShapeDtypeStruct((N, D), BF16),
                   jax.ShapeDtypeStruct((SUBLANES, N), jnp.int32),
                   jax.ShapeDtypeStruct((N, LANES), F32)],
        compiler_params=_cparams("parallel"),
        name="moe_router",
    )(x2, g.reshape(1, D), rw_hi, rw_lo, rb)


def _expert_kernel(te_ref, nu_ref, xs_ref, wg_ref, wu_ref, wd_ref, o_ref, acc_ref):
    t = pl.program_id(0)
    f = pl.program_id(1)
    used = t < nu_ref[0]

    @pl.when(f == 0)
    def _():
        acc_ref[...] = jnp.zeros_like(acc_ref)

    @pl.when(used)
    def _():
        xs = xs_ref[...]
        gate = _dot(xs, wg_ref[...].astype(BF16))
        up = _dot(xs, wu_ref[...].astype(BF16))
        act = (gate * _sigmoid(gate) * up).astype(BF16)
        acc_ref[...] += _dot(act, wd_ref[...].astype(BF16))

    @pl.when(f == pl.num_programs(1) - 1)
    def _():
        o_ref[...] = acc_ref[...].astype(BF16)


def _experts(xs, tile_expert, n_used, w_gate, w_up, w_down, j, tm, tf):
    S, D = xs.shape
    F = w_gate.shape[3]
    return pl.pallas_call(
        _expert_kernel,
        grid_spec=pltpu.PrefetchScalarGridSpec(
            num_scalar_prefetch=2,
            grid=(S // tm, F // tf),
            in_specs=[pl.BlockSpec((tm, D), lambda t, f, te, nu: (t, 0)),
                      pl.BlockSpec((None, None, D, tf), lambda t, f, te, nu: (j, te[t], 0, f)),
                      pl.BlockSpec((None, None, D, tf), lambda t, f, te, nu: (j, te[t], 0, f)),
                      pl.BlockSpec((None, None, tf, D), lambda t, f, te, nu: (j, te[t], f, 0))],
            out_specs=pl.BlockSpec((tm, D), lambda t, f, te, nu: (t, 0)),
            scratch_shapes=[pltpu.VMEM((tm, D), F32)]),
        out_shape=jax.ShapeDtypeStruct((S, D), BF16),
        compiler_params=_cparams("parallel", "arbitrary"),
        name="moe_experts",
    )(tile_expert, n_used, xs, w_gate, w_up, w_down)


def _route(idx, n_exp, tm):
    K, N = idx.shape
    experts = jnp.arange(n_exp, dtype=jnp.int32)
    onehot = (idx[:, None, :] == experts[None, :, None]).astype(jnp.int32)
    chosen = jnp.sum(onehot, axis=0)
    before = jnp.cumsum(chosen, axis=1) - chosen
    counts = jnp.sum(chosen, axis=1)
    padded = ((counts + tm - 1) // tm) * tm
    ends = jnp.cumsum(padded)
    starts = ends - padded
    slot = jnp.sum((starts[None, :, None] + before[None]) * onehot, axis=1).reshape(-1)
    S = N * K + n_exp * tm
    tile_start = jnp.arange(S // tm, dtype=jnp.int32) * tm
    last_start = jnp.maximum(ends[-1] - tm, 0)
    tile_expert = jnp.sum((jnp.minimum(tile_start, last_start)[:, None] >= ends[None, :])
                          .astype(jnp.int32), axis=1)
    tile_expert = jnp.minimum(tile_expert, n_exp - 1)
    n_used = (ends[-1] // tm).astype(jnp.int32).reshape(1)
    token = jnp.arange(N, dtype=jnp.int32)
    order = jnp.argsort((idx * N + token[None, :]).reshape(-1)).astype(jnp.int32)
    per_slot = lambda v: jnp.repeat(v[tile_expert], tm)
    within = jnp.arange(S, dtype=jnp.int32) - per_slot(starts)
    compact = jnp.clip(per_slot(jnp.cumsum(counts) - counts) + within, 0, N * K - 1)
    src = jnp.where(within < per_slot(counts), order[compact], jnp.arange(S, dtype=jnp.int32)) % N
    return src, slot.reshape(K, N), tile_expert, n_used


def _combine_kernel(x_ref, y0_ref, y1_ref, wgt_ref, g_ref, o_ref, *, final_norm):
    w0 = wgt_ref[:, 0:1]
    w1 = wgt_ref[:, 1:2]
    x = x_ref[...] + w0 * y0_ref[...].astype(F32) + w1 * y1_ref[...].astype(F32)
    o_ref[...] = _rms(x, g_ref[...]) if final_norm else x


def _combine(x2, y, wgt, g, tm, final_norm):
    N, D = x2.shape
    nt = N // tm
    return pl.pallas_call(
        functools.partial(_combine_kernel, final_norm=final_norm),
        grid=(nt,),
        in_specs=[pl.BlockSpec((tm, D), lambda i: (i, 0)),
                  pl.BlockSpec((tm, D), lambda i: (i, 0)),
                  pl.BlockSpec((tm, D), lambda i: (i + nt, 0)),
                  pl.BlockSpec((tm, LANES), lambda i: (i, 0)),
                  _resident((1, D))],
        out_specs=pl.BlockSpec((tm, D), lambda i: (i, 0)),
        out_shape=jax.ShapeDtypeStruct((N, D), F32),
        compiler_params=_cparams("parallel"),
        name="moe_combine",
    )(x2, y, y, wgt, g.reshape(1, D))


def _moe(x2, g, router_w, router_b, w_gate, w_up, w_down, j, final_g, tm_tok, tm_exp, tf):
    n_exp = router_w.shape[1]
    h, idx, wgt = _router(x2, g, router_w, router_b, tm_tok)
    src, slot, tile_expert, n_used = _route(idx[:TOP_K], n_exp, tm_exp)
    xs = jnp.take(h, src, axis=0, mode="clip")
    ys = _experts(xs, tile_expert, n_used, w_gate, w_up, w_down, j, tm_exp, tf)
    y = jnp.take(ys, slot.reshape(-1), axis=0, mode="clip")
    gn = final_g if final_g is not None else jnp.ones((x2.shape[1],), F32)
    return _combine(x2, y, wgt, gn, tm_tok, final_g is not None)


def _final_norm_kernel(x_ref, g_ref, o_ref):
    o_ref[...] = _rms(x_ref[...], g_ref[...])


def _final_norm(x2, g, tm):
    N, D = x2.shape
    return pl.pallas_call(
        _final_norm_kernel,
        grid=(N // tm,),
        in_specs=[pl.BlockSpec((tm, D), lambda i: (i, 0)), _resident((1, D))],
        out_specs=pl.BlockSpec((tm, D), lambda i: (i, 0)),
        out_shape=jax.ShapeDtypeStruct((N, D), F32),
        compiler_params=_cparams("parallel"),
        name="final_norm",
    )(x2, g.reshape(1, D))


def _tile(n, pref):
    t = min(n, pref)
    while n % t:
        t //= 2
    return t


def kernel(x, mix_norm_g, w_in, gate_b, s5_lambda_re, s5_lambda_im, s5_log_dt, s5_b_re, s5_b_im, s5_c_re, s5_c_im, s5_d, s5_w_glu, s5_b_glu, conv_w, conv_b, lru_w_a, lru_b_a, lru_w_x, lru_b_x, lru_lambda, w_branch, w_out, ffn_norm_g, ffn_w_gate, ffn_w_up, ffn_w_down, router_w, router_b, moe_w_gate, moe_w_up, moe_w_down, final_norm_g):
    B, T, D = x.shape
    depth = w_in.shape[0]
    W = MIX_WIDTH
    N = B * T
    tm = _tile(T, 512)
    tq = _tile(T, 256)
    tt = _tile(T, 128)
    L = _tile(T, S5_CHUNK)
    tm_exp = _tile(N, 1024)
    x = x.astype(F32)
    w_in_bf = w_in.astype(BF16)
    for layer in range(depth):
        main, kt, gates = _norm_proj(x, mix_norm_g[layer], w_in_bf, layer, conv_w[layer],
                                     conv_b[layer], tm, tq)
        def s5_with(chunk, rows, layer=layer):
            def run(main):
                consts = _s5_consts(s5_lambda_re[layer], s5_lambda_im[layer], s5_log_dt[layer],
                                    s5_b_re[layer], s5_b_im[layer], s5_c_re[layer],
                                    s5_c_im[layer], chunk)
                return _s5(main, consts, s5_d[layer], s5_w_glu[layer].astype(BF16),
                           s5_b_glu[layer], rows, chunk)
            return run

        step_decay = jnp.max(jnp.abs(s5_lambda_re[layer] * jnp.exp(s5_log_dt[layer])[:, None]))
        Ls = _tile(T, S5_SMALL_CHUNK)
        s5 = lax.cond(step_decay * (L // 2) < S5_SAFE_EXPONENT,
                      s5_with(L, tm), s5_with(Ls, _tile(T, 8 * Ls)), main)
        attn = _attention(main, kt, tq)
        lru = _rglru(main, lru_w_a[layer], lru_b_a[layer], lru_w_x[layer], lru_b_x[layer],
                     lru_lambda[layer], tt)
        x = _merge(s5, attn, lru, gates, x, w_branch[layer].astype(BF16),
                   gate_b[layer], w_out[layer].astype(BF16), tm)
        x2 = x.reshape(N, D)
        last = layer == depth - 1
        j = layer // 2
        if layer % 2 == 0:
            x2 = _ffn(x2, ffn_norm_g[layer], ffn_w_gate[j].astype(BF16), ffn_w_up[j].astype(BF16),
                      ffn_w_down[j].astype(BF16), tm, 3 * MXU_TILE)
            if last:
                x2 = _final_norm(x2, final_norm_g, tm)
        else:
            x2 = _moe(x2, ffn_norm_g[layer], router_w[j], router_b[j], moe_w_gate, moe_w_up,
                      moe_w_down, j, final_norm_g if last else None, tm, tm_exp,
                      _tile(moe_w_gate.shape[3], 2 * MXU_TILE))
        x = x2.reshape(B, T, D)
    return x
```

```python
import functools
import math

import jax
import jax.numpy as jnp
from jax import lax
from jax.experimental import pallas as pl
from jax.experimental.pallas import tpu as pltpu

F32 = jnp.float32
BF16 = jnp.bfloat16

RMS_EPS = 1e-6
MIX_WIDTH = 384
S5_GROUP = 16
S5_STATE = 64
S5_CHUNK = 128
S5_SMALL_CHUNK = 16
S5_MAX_EXPONENT = 80.0
S5_SAFE_EXPONENT = 60.0
SB_HEAD_DIM = 64
LRU_BLOCKS = 6
CONV_WIDTH = 4
LRU_C = 8.0
N_BRANCH = 3
TOP_K = 2
LANES = 128
SUBLANES = 8
MXU_TILE = 256
NEG_BIG = -1e30
LOG2E = 1.4426950408889634
POW2_ZERO_BELOW = -150.0
VMEM_LIMIT = 56 * 1024 * 1024


def _cparams(*sem):
    return pltpu.CompilerParams(dimension_semantics=sem, vmem_limit_bytes=VMEM_LIMIT)


def _resident(shape):
    n = len(shape)
    return pl.BlockSpec(shape, lambda *_: (0,) * n, pipeline_mode=pl.Buffered(1))


def _rms(xf, g):
    return xf * lax.rsqrt(jnp.mean(xf * xf, axis=-1, keepdims=True) + RMS_EPS) * g


def _gelu(x):
    c = math.sqrt(2.0 / math.pi)
    return 0.5 * x * (1.0 + jnp.tanh(c * (x + 0.044715 * (x * x * x))))


def _sigmoid(x):
    return 1.0 / (1.0 + jnp.exp(-x))


def _dot(a, b):
    return jnp.dot(a, b, preferred_element_type=F32)


def _norm_proj_kernel(x_ref, g_ref, w_ref, cw_ref, cb_ref, main_ref, kt_ref, gate_ref, tail_scr, *, w):
    tm = x_ref.shape[0]
    halo = tail_scr.shape[0]

    @pl.when(pl.program_id(1) == 0)
    def _():
        tail_scr[...] = jnp.zeros_like(tail_scr)

    h = _rms(x_ref[...], g_ref[...]).astype(BF16)
    split = -(-3 * w // MXU_TILE) * MXU_TILE
    rest = _dot(h, w_ref[:, split:6 * w])
    main_ref[:, split:4 * w] = rest[:, 0:4 * w - split].astype(BF16)
    first = _dot(h, w_ref[:, 0:split])
    main_ref[:, 0:split] = first.astype(BF16)
    xl = rest[:, 4 * w - split:5 * w - split]
    xx = jnp.concatenate([tail_scr[...], xl], axis=0)
    tail_scr[...] = xl[tm - halo:, :]
    xc = cb_ref[...]
    for j in range(CONV_WIDTH):
        off = halo - (CONV_WIDTH - 1) + j
        xc = xc + cw_ref[j:j + 1, :] * xx[off:off + tm, :]
    main_ref[:, 4 * w:5 * w] = xc.astype(BF16)
    main_ref[:, 5 * w:6 * w] = _gelu(rest[:, 5 * w - split:6 * w - split]).astype(BF16)
    kt = first[:, 2 * w:3 * w].T
    kb = kt_ref.shape[-1]
    for c in range(kt_ref.shape[0]):
        kt_ref[c] = kt[:, c * kb:(c + 1) * kb].astype(BF16)
    d = gate_ref.shape[-1] // N_BRANCH
    for n in range(N_BRANCH):
        c0 = 6 * w + n * d
        gate_ref[:, n * d:(n + 1) * d] = _dot(h, w_ref[:, c0:c0 + d]).astype(BF16)


def _norm_proj(x, g, w_in, layer, conv_w, conv_b, tm, kb):
    B, T, D = x.shape
    W = MIX_WIDTH
    nk = tm // kb
    halo = 8
    return pl.pallas_call(
        functools.partial(_norm_proj_kernel, w=W),
        grid=(B, T // tm),
        in_specs=[pl.BlockSpec((None, tm, D), lambda b, i: (b, i, 0)),
                  _resident((1, D)),
                  pl.BlockSpec((None,) + w_in.shape[1:], lambda b, i: (layer, 0, 0),
                               pipeline_mode=pl.Buffered(1)),
                  _resident((CONV_WIDTH, W)), _resident((1, W))],
        out_specs=[pl.BlockSpec((None, tm, 6 * W), lambda b, i: (b, i, 0)),
                   pl.BlockSpec((None, nk, W, kb), lambda b, i: (b, i, 0, 0)),
                   pl.BlockSpec((None, tm, N_BRANCH * D), lambda b, i: (b, i, 0))],
        out_shape=[jax.ShapeDtypeStruct((B, T, 6 * W), BF16),
                   jax.ShapeDtypeStruct((B, T // kb, W, kb), BF16),
                   jax.ShapeDtypeStruct((B, T, N_BRANCH * D), BF16)],
        scratch_shapes=[pltpu.VMEM((halo, W), F32)],
        compiler_params=_cparams("parallel", "arbitrary"),
        name="norm_proj",
    )(x, g.reshape(1, D), w_in, conv_w, conv_b.reshape(1, W))


def _s5_consts(lam_re, lam_im, log_dt, b_re, b_im, c_re, c_im, L):
    G, P, H = b_re.shape
    ns = (G * H) // LANES
    gs = G // ns
    dt = jnp.exp(log_dt)[:, None]
    ea = jnp.exp(lam_re * dt)
    abar_re = ea * jnp.cos(lam_im * dt)
    abar_im = ea * jnp.sin(lam_im * dt)
    den = lam_re * lam_re + lam_im * lam_im
    nr = abar_re - 1.0
    coef_re = (nr * lam_re + abar_im * lam_im) / den
    coef_im = (abar_im * lam_re - nr * lam_im) / den
    bb_re = coef_re[..., None] * b_re - coef_im[..., None] * b_im
    bb_im = coef_re[..., None] * b_im + coef_im[..., None] * b_re
    eye = jnp.eye(gs, dtype=F32)

    def in_slab(bb):
        return jnp.einsum('sgph,gk->sghkp', bb.reshape(ns, gs, P, H), eye).reshape(ns, gs * H, gs * P)

    def out_slab(c):
        return jnp.einsum('sgop,gk->sgpko', c.reshape(ns, gs, H, P), eye).reshape(ns, gs * P, gs * H)

    b_slab = jnp.concatenate([in_slab(bb_re), in_slab(bb_im)], axis=-1).astype(BF16)
    c_slab = jnp.concatenate([out_slab(c_re), -out_slab(c_im)], axis=1).astype(BF16)

    limit = S5_MAX_EXPONENT / (L // 2)
    log_mag = jnp.clip(lam_re * dt, -limit, limit)

    def power(k):
        mag = jnp.exp(log_mag[None] * k[:, None, None])
        ang = (lam_im * dt)[None] * k[:, None, None]
        lay = lambda v: v.reshape(k.shape[0], ns, gs * P).transpose(1, 0, 2)
        return lay(mag * jnp.cos(ang)), lay(mag * jnp.sin(ang))

    t = jnp.arange(L, dtype=F32)
    m = float(L // 2)
    em_re, em_im = power(m - t)
    ep_re, ep_im = power(t - m)
    vf_re, vf_im = power(jnp.full((1,), m + 1.0, F32))
    return b_slab, c_slab, em_re, em_im, ep_re, ep_im, vf_re, vf_im


def _s5_kernel(u_ref, b_ref, c_ref, emr_ref, emi_ref, epr_ref, epi_ref, vfr_ref, vfi_ref,
               d_ref, wglu_ref, bglu_ref, o_ref, carry_scr, s_scr, *, L):
    tc = u_ref.shape[0]
    ns = b_ref.shape[0]
    half = b_ref.shape[2] // 2

    @pl.when(pl.program_id(1) == 0)
    def _():
        carry_scr[...] = jnp.zeros_like(carry_scr)

    row = lax.broadcasted_iota(jnp.int32, (L, L), 0)
    col = lax.broadcasted_iota(jnp.int32, (L, L), 1)
    tri = (col <= row).astype(BF16)
    nc = tc // L
    bus = [_dot(u_ref[:, j * LANES:(j + 1) * LANES], b_ref[j]) for j in range(ns)]
    pres = []
    for j in range(ns):
        emr, emi = emr_ref[j], emi_ref[j]
        for c in range(nc):
            bur = bus[j][c * L:(c + 1) * L, :half]
            bui = bus[j][c * L:(c + 1) * L, half:]
            scaled = jnp.concatenate([bur * emr - bui * emi, bur * emi + bui * emr], axis=1)
            pres.append(_dot(tri, scaled.astype(BF16)))
    for j in range(ns):
        epr, epi, vfr, vfi = epr_ref[j], epi_ref[j], vfr_ref[j], vfi_ref[j]
        c_re = carry_scr[2 * j:2 * j + 1, :]
        c_im = carry_scr[2 * j + 1:2 * j + 2, :]
        for c in range(nc):
            pre = pres[j * nc + c]
            v_re = vfr * c_re - vfi * c_im
            v_im = vfr * c_im + vfi * c_re
            l_re = pre[L - 1:L, :half] + v_re
            l_im = pre[L - 1:L, half:] + v_im
            c_re = l_re * epr[L - 1:L, :] - l_im * epi[L - 1:L, :]
            c_im = l_re * epi[L - 1:L, :] + l_im * epr[L - 1:L, :]
            pr = pre[:, :half] + v_re
            pi = pre[:, half:] + v_im
            s_scr[j, c * L:(c + 1) * L, :] = jnp.concatenate(
                [pr * epr - pi * epi, pr * epi + pi * epr], axis=1).astype(BF16)
        carry_scr[2 * j:2 * j + 1, :] = c_re
        carry_scr[2 * j + 1:2 * j + 2, :] = c_im
    ys = [_dot(s_scr[j], c_ref[j]) for j in range(ns)]
    y = jnp.concatenate(ys, axis=1) + d_ref[...] * u_ref[...].astype(F32)
    ya = _gelu(y)
    o_ref[...] = (ya * _sigmoid(_dot(ya.astype(BF16), wglu_ref[...]) + bglu_ref[...])).astype(BF16)


def _s5(main, consts, d, w_glu, b_glu, tc, L):
    B, T, _ = main.shape
    W = MIX_WIDTH
    b_slab, c_slab = consts[0], consts[1]
    ns, _, two_half = b_slab.shape
    return pl.pallas_call(
        functools.partial(_s5_kernel, L=L),
        grid=(B, T // tc),
        in_specs=[pl.BlockSpec((None, tc, W), lambda b, i: (b, i, 0))]
        + [_resident(c.shape) for c in consts]
        + [_resident((1, W)), _resident((W, W)), _resident((1, W))],
        out_specs=pl.BlockSpec((None, tc, W), lambda b, i: (b, i, 0)),
        out_shape=jax.ShapeDtypeStruct((B, T, W), BF16),
        scratch_shapes=[pltpu.VMEM((2 * ns, two_half // 2), F32),
                        pltpu.VMEM((ns, tc, two_half), BF16)],
        compiler_params=_cparams("parallel", "arbitrary"),
        name="s5_mixer",
    )(main, *consts, d.reshape(1, W), w_glu, b_glu.reshape(1, W))


def _attn_kernel(q_ref, kt_ref, v_ref, o_ref, qs_scr, acc_scr, run_scr, *, tq, heads, dh):
    i = pl.program_id(1)
    row = lax.broadcasted_iota(jnp.int32, (tq, tq + LANES), 0)
    col = lax.broadcasted_iota(jnp.int32, (tq, tq + LANES), 1)
    suffix = jnp.logical_or(row > col, col >= tq).astype(BF16)
    causal = (lax.broadcasted_iota(jnp.int32, (tq, tq), 1)
              < lax.broadcasted_iota(jnp.int32, (tq, tq), 0))
    qs_scr[...] = (q_ref[...].astype(F32) * (dh ** -0.5 * LOG2E)).astype(BF16)

    hsl = [slice(h * dh, (h + 1) * dh) for h in range(heads)]

    def tile_pass(kb, diag):
        r0 = pl.multiple_of(kb * tq, tq)
        zs = [_dot(qs_scr[:, hs], kt_ref[kb, hs, :]) for hs in hsl]
        lbs, l1s = [], []
        for z in zs:
            sp = jnp.log(1.0 + jnp.exp2(-jnp.abs(z))) * LOG2E
            lb = jnp.minimum(z, 0.0) - sp
            l1 = lb - z
            lbs.append(lb)
            l1s.append(jnp.where(causal, l1, 0.0) if diag else l1)
        sums = [_dot(l1.astype(BF16), suffix) for l1 in l1s]
        wgts = []
        for h in range(heads):
            if diag:
                wgts.append(jnp.where(causal, jnp.exp2(lbs[h] + sums[h][:, :tq]), 0.0))
            else:
                run_wide = jnp.concatenate([run_scr[h]] * (tq // LANES), axis=1)
                wgts.append(jnp.exp2(lbs[h] + sums[h][:, :tq] + run_wide))
        for h, hs in enumerate(hsl):
            pv = _dot(wgts[h].astype(BF16), v_ref[pl.ds(r0, tq), hs])
            if diag:
                acc_scr[:, hs] = pv
                run_scr[h] = sums[h][:, tq:]
            else:
                acc_scr[:, hs] += pv
                run_scr[h] += sums[h][:, tq:]

    tile_pass(i, True)

    def top_run():
        top = run_scr[0]
        for h in range(1, heads):
            top = jnp.maximum(top, run_scr[h])
        return jnp.max(top)

    def cond(carry):
        j, top = carry
        return jnp.logical_and(j < i, top > POW2_ZERO_BELOW)

    def body(carry):
        j, _ = carry
        tile_pass(i - 1 - j, False)
        return j + 1, top_run()

    lax.while_loop(cond, body, (jnp.int32(0), top_run()))
    o_ref[...] = acc_scr[...].astype(BF16)


def _attention(main, kt, tq):
    B, T, _ = main.shape
    W = MIX_WIDTH
    heads = W // SB_HEAD_DIM
    return pl.pallas_call(
        functools.partial(_attn_kernel, tq=tq, heads=heads, dh=SB_HEAD_DIM),
        grid=(B, T // tq),
        in_specs=[pl.BlockSpec((None, tq, W), lambda b, i: (b, i, 1)),
                  pl.BlockSpec((None, T // tq, W, tq), lambda b, i: (b, 0, 0, 0)),
                  pl.BlockSpec((None, T, W), lambda b, i: (b, 0, 3))],
        out_specs=pl.BlockSpec((None, tq, W), lambda b, i: (b, i, 0)),
        out_shape=jax.ShapeDtypeStruct((B, T, W), BF16),
        scratch_shapes=[pltpu.VMEM((tq, W), BF16),
                        pltpu.VMEM((tq, W), F32),
                        pltpu.VMEM((heads, tq, LANES), F32)],
        compiler_params=_cparams("parallel", "arbitrary"),
        name="sb_attention",
    )(main, kt, main)


def _lru_kernel(x_ref, y_ref, wa_ref, ba_ref, wx_ref, bx_ref, lam_ref, o_ref,
                h_scr, a_scr, b_scr, *, pitch):
    nb, tt, w = x_ref.shape
    ns = w // LANES

    @pl.when(pl.program_id(0) == 0)
    def _():
        h_scr[...] = jnp.zeros_like(h_scr)

    xcb = x_ref[...].reshape(nb * tt, w)
    xc = xcb.astype(F32)
    r = _sigmoid(_dot(xcb, wa_ref[...]) + ba_ref[...])
    ig = _sigmoid(_dot(xcb, wx_ref[...]) + bx_ref[...])
    lam = lam_ref[...]
    log_sig_lam = jnp.minimum(lam, 0.0) - jnp.log(1.0 + jnp.exp(-jnp.abs(lam)))
    log_a = LRU_C * r * log_sig_lam
    a = jnp.exp(log_a)
    bb = jnp.sqrt(1.0 - a * a) * (ig * xc)
    for b in range(nb):
        for s in range(ns):
            a_scr[s, b * pitch:b * pitch + tt, :] = a[b * tt:(b + 1) * tt, s * LANES:(s + 1) * LANES]
            b_scr[s, b * pitch:b * pitch + tt, :] = bb[b * tt:(b + 1) * tt, s * LANES:(s + 1) * LANES]

    def step(t, h):
        rows_t = pl.ds(t, nb, stride=pitch)
        out = []
        for s in range(ns):
            hs = a_scr[s, rows_t, :] * h[s] + b_scr[s, rows_t, :]
            b_scr[s, rows_t, :] = hs
            out.append(hs)
        return tuple(out)

    h = lax.fori_loop(0, tt, step, tuple(h_scr[s] for s in range(ns)), unroll=8)
    for s in range(ns):
        h_scr[s] = h[s]
    for b in range(nb):
        for s in range(ns):
            cols = slice(s * LANES, (s + 1) * LANES)
            hb = b_scr[s, b * pitch:b * pitch + tt, :]
            o_ref[b, :, cols] = (hb * y_ref[b, :, cols].astype(F32)).astype(BF16)


def _block_diag(w):
    n, k, _ = w.shape
    eye = jnp.eye(n, dtype=w.dtype)
    return (eye[:, None, :, None] * w[:, :, None, :]).reshape(n * k, n * k)


def _rglru(main, w_a, b_a, w_x, b_x, lam, tt):
    B, T, _ = main.shape
    W = MIX_WIDTH
    pitch = tt + 8
    row = lambda v: v.reshape(1, W)
    return pl.pallas_call(
        functools.partial(_lru_kernel, pitch=pitch),
        grid=(T // tt,),
        in_specs=[pl.BlockSpec((B, tt, W), lambda i: (0, i, 4)),
                  pl.BlockSpec((B, tt, W), lambda i: (0, i, 5)),
                  _resident((W, W)), _resident((1, W)),
                  _resident((W, W)), _resident((1, W)), _resident((1, W))],
        out_specs=pl.BlockSpec((B, tt, W), lambda i: (0, i, 0)),
        out_shape=jax.ShapeDtypeStruct((B, T, W), BF16),
        scratch_shapes=[pltpu.VMEM((W // LANES, B, LANES), F32),
                        pltpu.VMEM((W // LANES, B * pitch, LANES), F32),
                        pltpu.VMEM((W // LANES, B * pitch, LANES), F32)],
        compiler_params=_cparams("arbitrary"),
        name="rglru",
    )(main, main, _block_diag(w_a).astype(BF16), row(b_a),
      _block_diag(w_x).astype(BF16), row(b_x), row(lam))


def _merge_kernel(s5_ref, at_ref, lr_ref, gate_ref, x_ref, wbr_ref, gb_ref, wout_ref, o_ref):
    dm = x_ref.shape[-1]
    branches = (s5_ref[...], at_ref[...], lr_ref[...])
    merged = None
    for n in range(N_BRANCH):
        gate = _sigmoid(gate_ref[:, n * dm:(n + 1) * dm].astype(F32) + gb_ref[:, n * dm:(n + 1) * dm])
        term = gate * _dot(branches[n], wbr_ref[n])
        merged = term if merged is None else merged + term
    o_ref[...] = x_ref[...] + _dot(merged.astype(BF16), wout_ref[...])


def _merge(s5, attn, lru, gates, x, w_branch, gate_b, w_out, tm):
    B, T, D = x.shape
    W = MIX_WIDTH
    tok = lambda width: pl.BlockSpec((None, tm, width), lambda b, i: (b, i, 0))
    return pl.pallas_call(
        _merge_kernel,
        grid=(B, T // tm),
        in_specs=[tok(W), tok(W), tok(W),
                  tok(N_BRANCH * D), tok(D),
                  _resident((N_BRANCH, W, D)), _resident((1, N_BRANCH * D)), _resident((D, D))],
        out_specs=tok(D),
        out_shape=jax.ShapeDtypeStruct((B, T, D), F32),
        compiler_params=_cparams("parallel", "parallel"),
        name="merge_out",
    )(s5, attn, lru, gates, x, w_branch, gate_b.reshape(1, N_BRANCH * D), w_out)


def _ffn_kernel(x_ref, g_ref, wg_ref, wu_ref, wd_ref, o_ref, *, tf):
    x = x_ref[...]
    h = _rms(x, g_ref[...]).astype(BF16)
    acc = x
    ff = wg_ref.shape[1]
    for c0 in range(0, ff, tf):
        c1 = min(c0 + tf, ff)
        gate = _dot(h, wg_ref[:, c0:c1])
        up = _dot(h, wu_ref[:, c0:c1])
        act = (gate * _sigmoid(gate) * up).astype(BF16)
        acc = acc + _dot(act, wd_ref[c0:c1, :])
    o_ref[...] = acc


def _ffn(x2, g, w_gate, w_up, w_down, tm, tf):
    N, D = x2.shape
    F = w_gate.shape[1]
    return pl.pallas_call(
        functools.partial(_ffn_kernel, tf=tf),
        grid=(N // tm,),
        in_specs=[pl.BlockSpec((tm, D), lambda i: (i, 0)), _resident((1, D)),
                  _resident((D, F)), _resident((D, F)), _resident((F, D))],
        out_specs=pl.BlockSpec((tm, D), lambda i: (i, 0)),
        out_shape=jax.ShapeDtypeStruct((N, D), F32),
        compiler_params=_cparams("parallel"),
        name="ffn_swiglu",
    )(x2, g.reshape(1, D), w_gate, w_up, w_down)


def _router_kernel(x_ref, g_ref, rwh_ref, rwl_ref, rb_ref, h_ref, idx_ref, rank_ref, cnt_ref, wgt_ref,
                   cnt_scr, *, n_exp):
    tm = x_ref.shape[0]

    @pl.when(pl.program_id(0) == 0)
    def _():
        cnt_scr[...] = jnp.zeros_like(cnt_scr)

    h = _rms(x_ref[...], g_ref[...])
    h_hi = h.astype(BF16)
    h_ref[...] = h_hi
    h_lo = (h - h_hi.astype(F32)).astype(BF16)
    logits = _dot(h_hi, rwh_ref[...]) + _dot(h_lo, rwh_ref[...]) + _dot(h_hi, rwl_ref[...])
    lane = lax.broadcasted_iota(jnp.int32, logits.shape, 1)
    lg = jnp.where(lane < n_exp, logits + rb_ref[...], NEG_BIG)
    m1 = jnp.max(lg, axis=1, keepdims=True)
    i1 = jnp.min(jnp.where(lg == m1, lane, LANES), axis=1, keepdims=True)
    lg2 = jnp.where(lane == i1, NEG_BIG, lg)
    m2 = jnp.max(lg2, axis=1, keepdims=True)
    i2 = jnp.min(jnp.where(lg2 == m2, lane, LANES), axis=1, keepdims=True)
    e = jnp.exp(m2 - m1)
    w1 = 1.0 / (1.0 + e)
    w2 = e / (1.0 + e)
    idx_t = jnp.where(lane == 0, i1, jnp.where(lane == 1, i2, 0)).T[0:SUBLANES, :]
    idx_ref[...] = idx_t
    wgt_ref[...] = jnp.where(lane == 0, w1, jnp.where(lane == 1, w2, 0.0))
    sub = lax.broadcasted_iota(jnp.int32, (SUBLANES, tm), 0)
    pick1 = sub == idx_t[0:1, :]
    pick2 = sub == idx_t[1:2, :]
    chosen = jnp.logical_or(pick1, pick2).astype(BF16)
    earlier = (lax.broadcasted_iota(jnp.int32, (tm, tm), 0)
               < lax.broadcasted_iota(jnp.int32, (tm, tm), 1)).astype(BF16)
    before = _dot(chosen, earlier) + jnp.concatenate([cnt_scr[...]] * (tm // LANES), axis=1)
    rank1 = jnp.sum(jnp.where(pick1, before, 0.0), axis=0, keepdims=True)
    rank2 = jnp.sum(jnp.where(pick2, before, 0.0), axis=0, keepdims=True)
    rank_ref[...] = jnp.where(sub == 0, rank1, jnp.where(sub == 1, rank2, 0.0)).astype(jnp.int32)
    cnt_scr[...] += _dot(chosen, jnp.ones((tm, LANES), BF16))
    cnt_ref[...] = cnt_scr[...].astype(jnp.int32)


def _router(x2, g, router_w, router_b, tm):
    N, D = x2.shape
    E = router_w.shape[1]
    assert E <= SUBLANES, "experts are laid out on the sublanes of one vreg"
    rw = jnp.zeros((D, LANES), F32).at[:, :E].set(router_w)
    rw_hi = rw.astype(BF16)
    rw_lo = (rw - rw_hi.astype(F32)).astype(BF16)
    rb = jnp.zeros((1, LANES), F32).at[0, :E].set(router_b)
    return pl.pallas_call(
        functools.partial(_router_kernel, n_exp=E),
        grid=(N // tm,),
        in_specs=[pl.BlockSpec((tm, D), lambda i: (i, 0)), _resident((1, D)),
                  _resident((D, LANES)), _resident((D, LANES)), _resident((1, LANES))],
        out_specs=[pl.BlockSpec((tm, D), lambda i: (i, 0)),
                   pl.BlockSpec((SUBLANES, tm), lambda i: (0, i)),
                   pl.BlockSpec((SUBLANES, tm), lambda i: (0, i)),
                   pl.BlockSpec((SUBLANES, LANES), lambda i: (0, 0)),
                   pl.BlockSpec((tm, LANES), lambda i: (i, 0))],
        out_shape=[jax.ShapeDtypeStruct((N, D), BF16),
                   jax.ShapeDtypeStruct((SUBLANES, N), jnp.int32),
                   jax.ShapeDtypeStruct((SUBLANES, N), jnp.int32),
                   jax.ShapeDtypeStruct((SUBLANES, LANES), jnp.int32),
                   jax.ShapeDtypeStruct((N, LANES), F32)],
        scratch_shapes=[pltpu.VMEM((SUBLANES, LANES), F32)],
        compiler_params=_cparams("arbitrary"),
        name="moe_router",
    )(x2, g.reshape(1, D), rw_hi, rw_lo, rb)


def _expert_kernel(te_ref, nu_ref, xs_ref, wg_ref, wu_ref, wd_ref, o_ref, acc_ref):
    t = pl.program_id(0)
    f = pl.program_id(1)
    used = t < nu_ref[0]

    @pl.when(f == 0)
    def _():
        acc_ref[...] = jnp.zeros_like(acc_ref)

    @pl.when(used)
    def _():
        xs = xs_ref[...]
        gate = _dot(xs, wg_ref[...].astype(BF16))
        up = _dot(xs, wu_ref[...].astype(BF16))
        act = (gate * _sigmoid(gate) * up).astype(BF16)
        acc_ref[...] += _dot(act, wd_ref[...].astype(BF16))

    @pl.when(f == pl.num_programs(1) - 1)
    def _():
        o_ref[...] = acc_ref[...].astype(BF16)


def _experts(xs, tile_expert, n_used, w_gate, w_up, w_down, j, tm, tf):
    S, D = xs.shape
    F = w_gate.shape[3]
    return pl.pallas_call(
        _expert_kernel,
        grid_spec=pltpu.PrefetchScalarGridSpec(
            num_scalar_prefetch=2,
            grid=(S // tm, F // tf),
            in_specs=[pl.BlockSpec((tm, D), lambda t, f, te, nu: (t, 0)),
                      pl.BlockSpec((None, None, D, tf), lambda t, f, te, nu: (j, te[t], 0, f)),
                      pl.BlockSpec((None, None, D, tf), lambda t, f, te, nu: (j, te[t], 0, f)),
                      pl.BlockSpec((None, None, tf, D), lambda t, f, te, nu: (j, te[t], f, 0))],
            out_specs=pl.BlockSpec((tm, D), lambda t, f, te, nu: (t, 0)),
            scratch_shapes=[pltpu.VMEM((tm, D), F32)]),
        out_shape=jax.ShapeDtypeStruct((S, D), BF16),
        compiler_params=_cparams("parallel", "arbitrary"),
        name="moe_experts",
    )(tile_expert, n_used, xs, w_gate, w_up, w_down)


def _route(idx, rank, counts, tm):
    K, N = idx.shape
    n_exp = counts.shape[0]
    padded = ((counts + tm - 1) // tm) * tm
    ends = jnp.cumsum(padded)
    starts = ends - padded
    start_of = jnp.sum(jnp.where(idx[None] == jnp.arange(n_exp, dtype=jnp.int32)[:, None, None],
                                 starts[:, None, None], 0), axis=0)
    slot = (start_of + rank).reshape(-1)
    S = N * K + n_exp * tm
    tile_start = jnp.arange(S // tm, dtype=jnp.int32) * tm
    last_start = jnp.maximum(ends[-1] - tm, 0)
    tile_expert = jnp.sum((jnp.minimum(tile_start, last_start)[:, None] >= ends[None, :])
                          .astype(jnp.int32), axis=1)
    tile_expert = jnp.minimum(tile_expert, n_exp - 1)
    n_used = (ends[-1] // tm).astype(jnp.int32).reshape(1)
    token = jnp.arange(N, dtype=jnp.int32)
    order = jnp.argsort((idx * N + token[None, :]).reshape(-1)).astype(jnp.int32)
    per_slot = lambda v: jnp.repeat(v[tile_expert], tm)
    within = jnp.arange(S, dtype=jnp.int32) - per_slot(starts)
    compact = jnp.clip(per_slot(jnp.cumsum(counts) - counts) + within, 0, N * K - 1)
    src = jnp.where(within < per_slot(counts), order[compact], jnp.arange(S, dtype=jnp.int32)) % N
    return src, slot.reshape(K, N), tile_expert, n_used


def _combine_kernel(x_ref, y0_ref, y1_ref, wgt_ref, g_ref, o_ref, *, final_norm):
    w0 = wgt_ref[:, 0:1]
    w1 = wgt_ref[:, 1:2]
    x = x_ref[...] + w0 * y0_ref[...].astype(F32) + w1 * y1_ref[...].astype(F32)
    o_ref[...] = _rms(x, g_ref[...]) if final_norm else x


def _combine(x2, y, wgt, g, tm, final_norm):
    N, D = x2.shape
    nt = N // tm
    return pl.pallas_call(
        functools.partial(_combine_kernel, final_norm=final_norm),
        grid=(nt,),
        in_specs=[pl.BlockSpec((tm, D), lambda i: (i, 0)),
                  pl.BlockSpec((tm, D), lambda i: (i, 0)),
                  pl.BlockSpec((tm, D), lambda i: (i + nt, 0)),
                  pl.BlockSpec((tm, LANES), lambda i: (i, 0)),
                  _resident((1, D))],
        out_specs=pl.BlockSpec((tm, D), lambda i: (i, 0)),
        out_shape=jax.ShapeDtypeStruct((N, D), F32),
        compiler_params=_cparams("parallel"),
        name="moe_combine",
    )(x2, y, y, wgt, g.reshape(1, D))


def _moe(x2, g, router_w, router_b, w_gate, w_up, w_down, j, final_g, tm_tok, tm_exp, tf):
    n_exp = router_w.shape[1]
    h, idx, rank, cnt, wgt = _router(x2, g, router_w, router_b, tm_tok)
    src, slot, tile_expert, n_used = _route(idx[:TOP_K], rank[:TOP_K], cnt[:n_exp, 0], tm_exp)
    xs = jnp.take(h, src, axis=0, mode="clip")
    ys = _experts(xs, tile_expert, n_used, w_gate, w_up, w_down, j, tm_exp, tf)
    y = jnp.take(ys, slot.reshape(-1), axis=0, mode="clip")
    gn = final_g if final_g is not None else jnp.ones((x2.shape[1],), F32)
    return _combine(x2, y, wgt, gn, tm_tok, final_g is not None)


def _final_norm_kernel(x_ref, g_ref, o_ref):
    o_ref[...] = _rms(x_ref[...], g_ref[...])


def _final_norm(x2, g, tm):
    N, D = x2.shape
    return pl.pallas_call(
        _final_norm_kernel,
        grid=(N // tm,),
        in_specs=[pl.BlockSpec((tm, D), lambda i: (i, 0)), _resident((1, D))],
        out_specs=pl.BlockSpec((tm, D), lambda i: (i, 0)),
        out_shape=jax.ShapeDtypeStruct((N, D), F32),
        compiler_params=_cparams("parallel"),
        name="final_norm",
    )(x2, g.reshape(1, D))


def _tile(n, pref):
    t = min(n, pref)
    while n % t:
        t //= 2
    return t


def kernel(x, mix_norm_g, w_in, gate_b, s5_lambda_re, s5_lambda_im, s5_log_dt, s5_b_re, s5_b_im, s5_c_re, s5_c_im, s5_d, s5_w_glu, s5_b_glu, conv_w, conv_b, lru_w_a, lru_b_a, lru_w_x, lru_b_x, lru_lambda, w_branch, w_out, ffn_norm_g, ffn_w_gate, ffn_w_up, ffn_w_down, router_w, router_b, moe_w_gate, moe_w_up, moe_w_down, final_norm_g):
    B, T, D = x.shape
    depth = w_in.shape[0]
    W = MIX_WIDTH
    N = B * T
    tm = _tile(T, 512)
    tq = _tile(T, 256)
    tt = _tile(T, 128)
    L = _tile(T, S5_CHUNK)
    tm_exp = _tile(N, 1024)
    x = x.astype(F32)
    w_in_bf = w_in.astype(BF16)
    for layer in range(depth):
        main, kt, gates = _norm_proj(x, mix_norm_g[layer], w_in_bf, layer, conv_w[layer],
                                     conv_b[layer], tm, tq)
        def s5_with(chunk, rows, layer=layer):
            def run(main):
                consts = _s5_consts(s5_lambda_re[layer], s5_lambda_im[layer], s5_log_dt[layer],
                                    s5_b_re[layer], s5_b_im[layer], s5_c_re[layer],
                                    s5_c_im[layer], chunk)
                return _s5(main, consts, s5_d[layer], s5_w_glu[layer].astype(BF16),
                           s5_b_glu[layer], rows, chunk)
            return run

        step_decay = jnp.max(jnp.abs(s5_lambda_re[layer] * jnp.exp(s5_log_dt[layer])[:, None]))
        Ls = _tile(T, S5_SMALL_CHUNK)
        s5 = lax.cond(step_decay * (L // 2) < S5_SAFE_EXPONENT,
                      s5_with(L, tm), s5_with(Ls, _tile(T, 8 * Ls)), main)
        attn = _attention(main, kt, tq)
        lru = _rglru(main, lru_w_a[layer], lru_b_a[layer], lru_w_x[layer], lru_b_x[layer],
                     lru_lambda[layer], tt)
        x = _merge(s5, attn, lru, gates, x, w_branch[layer].astype(BF16),
                   gate_b[layer], w_out[layer].astype(BF16), tm)
        x2 = x.reshape(N, D)
        last = layer == depth - 1
        j = layer // 2
        if layer % 2 == 0:
            x2 = _ffn(x2, ffn_norm_g[layer], ffn_w_gate[j].astype(BF16), ffn_w_up[j].astype(BF16),
                      ffn_w_down[j].astype(BF16), tm, 3 * MXU_TILE)
            if last:
                x2 = _final_norm(x2, final_norm_g, tm)
        else:
            x2 = _moe(x2, ffn_norm_g[layer], router_w[j], router_b[j], moe_w_gate, moe_w_up,
                      moe_w_down, j, final_norm_g if last else None, tm, tm_exp,
                      _tile(moe_w_gate.shape[3], 2 * MXU_TILE))
        x = x2.reshape(B, T, D)
    return x
```

```python
import functools
import math

import jax
import jax.numpy as jnp
from jax import lax
from jax.experimental import pallas as pl
from jax.experimental.pallas import tpu as pltpu

F32 = jnp.float32
BF16 = jnp.bfloat16

RMS_EPS = 1e-6
MIX_WIDTH = 384
S5_GROUP = 16
S5_STATE = 64
S5_CHUNK = 128
S5_SMALL_CHUNK = 16
S5_MAX_EXPONENT = 80.0
S5_SAFE_EXPONENT = 60.0
SB_HEAD_DIM = 64
LRU_BLOCKS = 6
CONV_WIDTH = 4
LRU_C = 8.0
N_BRANCH = 3
TOP_K = 2
LANES = 128
SUBLANES = 8
MXU_TILE = 256
NEG_BIG = -1e30
LOG2E = 1.4426950408889634
POW2_ZERO_BELOW = -150.0
VMEM_LIMIT = 56 * 1024 * 1024


def _cparams(*sem):
    return pltpu.CompilerParams(dimension_semantics=sem, vmem_limit_bytes=VMEM_LIMIT)


def _resident(shape):
    n = len(shape)
    return pl.BlockSpec(shape, lambda *_: (0,) * n, pipeline_mode=pl.Buffered(1))


def _rms(xf, g):
    return xf * lax.rsqrt(jnp.mean(xf * xf, axis=-1, keepdims=True) + RMS_EPS) * g


def _gelu(x):
    c = math.sqrt(2.0 / math.pi)
    return 0.5 * x * (1.0 + jnp.tanh(c * (x + 0.044715 * (x * x * x))))


def _sigmoid(x):
    return 1.0 / (1.0 + jnp.exp(-x))


def _dot(a, b):
    return jnp.dot(a, b, preferred_element_type=F32)


def _norm_proj_kernel(x_ref, g_ref, w_ref, cw_ref, cb_ref, main_ref, kt_ref, gate_ref, tail_scr, *, w):
    tm = x_ref.shape[0]
    halo = tail_scr.shape[0]

    @pl.when(pl.program_id(1) == 0)
    def _():
        tail_scr[...] = jnp.zeros_like(tail_scr)

    h = _rms(x_ref[...], g_ref[...]).astype(BF16)
    split = -(-3 * w // MXU_TILE) * MXU_TILE
    rest = _dot(h, w_ref[:, split:6 * w])
    main_ref[:, split:4 * w] = rest[:, 0:4 * w - split].astype(BF16)
    first = _dot(h, w_ref[:, 0:split])
    main_ref[:, 0:split] = first.astype(BF16)
    xl = rest[:, 4 * w - split:5 * w - split]
    xx = jnp.concatenate([tail_scr[...], xl], axis=0)
    tail_scr[...] = xl[tm - halo:, :]
    xc = cb_ref[...]
    for j in range(CONV_WIDTH):
        off = halo - (CONV_WIDTH - 1) + j
        xc = xc + cw_ref[j:j + 1, :] * xx[off:off + tm, :]
    main_ref[:, 4 * w:5 * w] = xc.astype(BF16)
    main_ref[:, 5 * w:6 * w] = _gelu(rest[:, 5 * w - split:6 * w - split]).astype(BF16)
    kt = first[:, 2 * w:3 * w].T
    kb = kt_ref.shape[-1]
    for c in range(kt_ref.shape[0]):
        kt_ref[c] = kt[:, c * kb:(c + 1) * kb].astype(BF16)
    d = gate_ref.shape[-1] // N_BRANCH
    for n in range(N_BRANCH):
        c0 = 6 * w + n * d
        gate_ref[:, n * d:(n + 1) * d] = _dot(h, w_ref[:, c0:c0 + d]).astype(BF16)


def _norm_proj(x, g, w_in, layer, conv_w, conv_b, tm, kb):
    B, T, D = x.shape
    W = MIX_WIDTH
    nk = tm // kb
    halo = 8
    return pl.pallas_call(
        functools.partial(_norm_proj_kernel, w=W),
        grid=(B, T // tm),
        in_specs=[pl.BlockSpec((None, tm, D), lambda b, i: (b, i, 0)),
                  _resident((1, D)),
                  pl.BlockSpec((None,) + w_in.shape[1:], lambda b, i: (layer, 0, 0),
                               pipeline_mode=pl.Buffered(1)),
                  _resident((CONV_WIDTH, W)), _resident((1, W))],
        out_specs=[pl.BlockSpec((None, tm, 6 * W), lambda b, i: (b, i, 0)),
                   pl.BlockSpec((None, nk, W, kb), lambda b, i: (b, i, 0, 0)),
                   pl.BlockSpec((None, tm, N_BRANCH * D), lambda b, i: (b, i, 0))],
        out_shape=[jax.ShapeDtypeStruct((B, T, 6 * W), BF16),
                   jax.ShapeDtypeStruct((B, T // kb, W, kb), BF16),
                   jax.ShapeDtypeStruct((B, T, N_BRANCH * D), BF16)],
        scratch_shapes=[pltpu.VMEM((halo, W), F32)],
        compiler_params=_cparams("parallel", "arbitrary"),
        name="norm_proj",
    )(x, g.reshape(1, D), w_in, conv_w, conv_b.reshape(1, W))


def _s5_consts(lam_re, lam_im, log_dt, b_re, b_im, c_re, c_im, L):
    G, P, H = b_re.shape
    ns = (G * H) // LANES
    gs = G // ns
    dt = jnp.exp(log_dt)[:, None]
    ea = jnp.exp(lam_re * dt)
    abar_re = ea * jnp.cos(lam_im * dt)
    abar_im = ea * jnp.sin(lam_im * dt)
    den = lam_re * lam_re + lam_im * lam_im
    nr = abar_re - 1.0
    coef_re = (nr * lam_re + abar_im * lam_im) / den
    coef_im = (abar_im * lam_re - nr * lam_im) / den
    bb_re = coef_re[..., None] * b_re - coef_im[..., None] * b_im
    bb_im = coef_re[..., None] * b_im + coef_im[..., None] * b_re
    eye = jnp.eye(gs, dtype=F32)

    def in_slab(bb):
        return jnp.einsum('sgph,gk->sghkp', bb.reshape(ns, gs, P, H), eye).reshape(ns, gs * H, gs * P)

    def out_slab(c):
        return jnp.einsum('sgop,gk->sgpko', c.reshape(ns, gs, H, P), eye).reshape(ns, gs * P, gs * H)

    b_slab = jnp.concatenate([in_slab(bb_re), in_slab(bb_im)], axis=-1).astype(BF16)
    c_slab = jnp.concatenate([out_slab(c_re), -out_slab(c_im)], axis=1).astype(BF16)

    limit = S5_MAX_EXPONENT / (L // 2)
    log_mag = jnp.clip(lam_re * dt, -limit, limit)

    def power(k):
        mag = jnp.exp(log_mag[None] * k[:, None, None])
        ang = (lam_im * dt)[None] * k[:, None, None]
        lay = lambda v: v.reshape(k.shape[0], ns, gs * P).transpose(1, 0, 2)
        return lay(mag * jnp.cos(ang)), lay(mag * jnp.sin(ang))

    t = jnp.arange(L, dtype=F32)
    m = float(L // 2)
    em_re, em_im = power(m - t)
    ep_re, ep_im = power(t - m)
    vf_re, vf_im = power(jnp.full((1,), m + 1.0, F32))
    return b_slab, c_slab, em_re, em_im, ep_re, ep_im, vf_re, vf_im


def _s5_kernel(u_ref, b_ref, c_ref, emr_ref, emi_ref, epr_ref, epi_ref, vfr_ref, vfi_ref,
               d_ref, wglu_ref, bglu_ref, o_ref, carry_scr, s_scr, *, L):
    tc = u_ref.shape[0]
    ns = b_ref.shape[0]
    half = b_ref.shape[2] // 2

    @pl.when(pl.program_id(1) == 0)
    def _():
        carry_scr[...] = jnp.zeros_like(carry_scr)

    row = lax.broadcasted_iota(jnp.int32, (L, L), 0)
    col = lax.broadcasted_iota(jnp.int32, (L, L), 1)
    tri = (col <= row).astype(BF16)
    nc = tc // L
    bus = [_dot(u_ref[:, j * LANES:(j + 1) * LANES], b_ref[j]) for j in range(ns)]
    pres = []
    for j in range(ns):
        emr, emi = emr_ref[j], emi_ref[j]
        for c in range(nc):
            bur = bus[j][c * L:(c + 1) * L, :half]
            bui = bus[j][c * L:(c + 1) * L, half:]
            scaled = jnp.concatenate([bur * emr - bui * emi, bur * emi + bui * emr], axis=1)
            pres.append(_dot(tri, scaled.astype(BF16)))
    for j in range(ns):
        epr, epi, vfr, vfi = epr_ref[j], epi_ref[j], vfr_ref[j], vfi_ref[j]
        c_re = carry_scr[2 * j:2 * j + 1, :]
        c_im = carry_scr[2 * j + 1:2 * j + 2, :]
        for c in range(nc):
            pre = pres[j * nc + c]
            v_re = vfr * c_re - vfi * c_im
            v_im = vfr * c_im + vfi * c_re
            l_re = pre[L - 1:L, :half] + v_re
            l_im = pre[L - 1:L, half:] + v_im
            c_re = l_re * epr[L - 1:L, :] - l_im * epi[L - 1:L, :]
            c_im = l_re * epi[L - 1:L, :] + l_im * epr[L - 1:L, :]
            pr = pre[:, :half] + v_re
            pi = pre[:, half:] + v_im
            s_scr[j, c * L:(c + 1) * L, :] = jnp.concatenate(
                [pr * epr - pi * epi, pr * epi + pi * epr], axis=1).astype(BF16)
        carry_scr[2 * j:2 * j + 1, :] = c_re
        carry_scr[2 * j + 1:2 * j + 2, :] = c_im
    ys = [_dot(s_scr[j], c_ref[j]) for j in range(ns)]
    y = jnp.concatenate(ys, axis=1) + d_ref[...] * u_ref[...].astype(F32)
    ya = _gelu(y)
    o_ref[...] = (ya * _sigmoid(_dot(ya.astype(BF16), wglu_ref[...]) + bglu_ref[...])).astype(BF16)


def _s5(main, consts, d, w_glu, b_glu, tc, L):
    B, T, _ = main.shape
    W = MIX_WIDTH
    b_slab, c_slab = consts[0], consts[1]
    ns, _, two_half = b_slab.shape
    return pl.pallas_call(
        functools.partial(_s5_kernel, L=L),
        grid=(B, T // tc),
        in_specs=[pl.BlockSpec((None, tc, W), lambda b, i: (b, i, 0))]
        + [_resident(c.shape) for c in consts]
        + [_resident((1, W)), _resident((W, W)), _resident((1, W))],
        out_specs=pl.BlockSpec((None, tc, W), lambda b, i: (b, i, 0)),
        out_shape=jax.ShapeDtypeStruct((B, T, W), BF16),
        scratch_shapes=[pltpu.VMEM((2 * ns, two_half // 2), F32),
                        pltpu.VMEM((ns, tc, two_half), BF16)],
        compiler_params=_cparams("parallel", "arbitrary"),
        name="s5_mixer",
    )(main, *consts, d.reshape(1, W), w_glu, b_glu.reshape(1, W))


def _attn_kernel(q_ref, kt_ref, v_ref, o_ref, qs_scr, acc_scr, run_scr, *, tq, heads, dh):
    i = pl.program_id(1)
    row = lax.broadcasted_iota(jnp.int32, (tq, tq + LANES), 0)
    col = lax.broadcasted_iota(jnp.int32, (tq, tq + LANES), 1)
    suffix = jnp.logical_or(row > col, col >= tq).astype(BF16)
    causal = (lax.broadcasted_iota(jnp.int32, (tq, tq), 1)
              < lax.broadcasted_iota(jnp.int32, (tq, tq), 0))
    qs_scr[...] = (q_ref[...].astype(F32) * (dh ** -0.5 * LOG2E)).astype(BF16)

    hsl = [slice(h * dh, (h + 1) * dh) for h in range(heads)]

    def tile_pass(kb, diag):
        r0 = pl.multiple_of(kb * tq, tq)
        zs = [_dot(qs_scr[:, hs], kt_ref[kb, hs, :]) for hs in hsl]
        lbs, l1s = [], []
        for z in zs:
            sp = jnp.log(1.0 + jnp.exp2(-jnp.abs(z))) * LOG2E
            lb = jnp.minimum(z, 0.0) - sp
            l1 = lb - z
            lbs.append(lb)
            l1s.append(jnp.where(causal, l1, 0.0) if diag else l1)
        sums = [_dot(l1.astype(BF16), suffix) for l1 in l1s]
        wgts = []
        for h in range(heads):
            if diag:
                wgts.append(jnp.where(causal, jnp.exp2(lbs[h] + sums[h][:, :tq]), 0.0))
            else:
                run_wide = jnp.concatenate([run_scr[h]] * (tq // LANES), axis=1)
                wgts.append(jnp.exp2(lbs[h] + sums[h][:, :tq] + run_wide))
        for h, hs in enumerate(hsl):
            pv = _dot(wgts[h].astype(BF16), v_ref[pl.ds(r0, tq), hs])
            if diag:
                acc_scr[:, hs] = pv
                run_scr[h] = sums[h][:, tq:]
            else:
                acc_scr[:, hs] += pv
                run_scr[h] += sums[h][:, tq:]

    tile_pass(i, True)

    def top_run():
        top = run_scr[0]
        for h in range(1, heads):
            top = jnp.maximum(top, run_scr[h])
        return jnp.max(top)

    def cond(carry):
        j, top = carry
        return jnp.logical_and(j < i, top > POW2_ZERO_BELOW)

    def body(carry):
        j, _ = carry
        tile_pass(i - 1 - j, False)
        return j + 1, top_run()

    lax.while_loop(cond, body, (jnp.int32(0), top_run()))
    o_ref[...] = acc_scr[...].astype(BF16)


def _attention(main, kt, tq):
    B, T, _ = main.shape
    W = MIX_WIDTH
    heads = W // SB_HEAD_DIM
    return pl.pallas_call(
        functools.partial(_attn_kernel, tq=tq, heads=heads, dh=SB_HEAD_DIM),
        grid=(B, T // tq),
        in_specs=[pl.BlockSpec((None, tq, W), lambda b, i: (b, i, 1)),
                  pl.BlockSpec((None, T // tq, W, tq), lambda b, i: (b, 0, 0, 0)),
                  pl.BlockSpec((None, T, W), lambda b, i: (b, 0, 3))],
        out_specs=pl.BlockSpec((None, tq, W), lambda b, i: (b, i, 0)),
        out_shape=jax.ShapeDtypeStruct((B, T, W), BF16),
        scratch_shapes=[pltpu.VMEM((tq, W), BF16),
                        pltpu.VMEM((tq, W), F32),
                        pltpu.VMEM((heads, tq, LANES), F32)],
        compiler_params=_cparams("parallel", "arbitrary"),
        name="sb_attention",
    )(main, kt, main)


def _lru_kernel(x_ref, y_ref, wa_ref, ba_ref, wx_ref, bx_ref, lam_ref, o_ref,
                h_scr, a_scr, b_scr, *, pitch):
    nb, tt, w = x_ref.shape
    ns = w // LANES

    @pl.when(pl.program_id(0) == 0)
    def _():
        h_scr[...] = jnp.zeros_like(h_scr)

    xcb = x_ref[...].reshape(nb * tt, w)
    xc = xcb.astype(F32)
    r = _sigmoid(_dot(xcb, wa_ref[...]) + ba_ref[...])
    ig = _sigmoid(_dot(xcb, wx_ref[...]) + bx_ref[...])
    lam = lam_ref[...]
    log_sig_lam = jnp.minimum(lam, 0.0) - jnp.log(1.0 + jnp.exp(-jnp.abs(lam)))
    log_a = LRU_C * r * log_sig_lam
    a = jnp.exp(log_a)
    bb = jnp.sqrt(1.0 - a * a) * (ig * xc)
    for b in range(nb):
        for s in range(ns):
            a_scr[s, b * pitch:b * pitch + tt, :] = a[b * tt:(b + 1) * tt, s * LANES:(s + 1) * LANES]
            b_scr[s, b * pitch:b * pitch + tt, :] = bb[b * tt:(b + 1) * tt, s * LANES:(s + 1) * LANES]

    def step(t, h):
        rows_t = pl.ds(t, nb, stride=pitch)
        out = []
        for s in range(ns):
            hs = a_scr[s, rows_t, :] * h[s] + b_scr[s, rows_t, :]
            b_scr[s, rows_t, :] = hs
            out.append(hs)
        return tuple(out)

    h = lax.fori_loop(0, tt, step, tuple(h_scr[s] for s in range(ns)), unroll=8)
    for s in range(ns):
        h_scr[s] = h[s]
    for b in range(nb):
        for s in range(ns):
            cols = slice(s * LANES, (s + 1) * LANES)
            hb = b_scr[s, b * pitch:b * pitch + tt, :]
            o_ref[b, :, cols] = (hb * y_ref[b, :, cols].astype(F32)).astype(BF16)


def _block_diag(w):
    n, k, _ = w.shape
    eye = jnp.eye(n, dtype=w.dtype)
    return (eye[:, None, :, None] * w[:, :, None, :]).reshape(n * k, n * k)


def _rglru(main, w_a, b_a, w_x, b_x, lam, tt):
    B, T, _ = main.shape
    W = MIX_WIDTH
    pitch = tt + 8
    row = lambda v: v.reshape(1, W)
    return pl.pallas_call(
        functools.partial(_lru_kernel, pitch=pitch),
        grid=(T // tt,),
        in_specs=[pl.BlockSpec((B, tt, W), lambda i: (0, i, 4)),
                  pl.BlockSpec((B, tt, W), lambda i: (0, i, 5)),
                  _resident((W, W)), _resident((1, W)),
                  _resident((W, W)), _resident((1, W)), _resident((1, W))],
        out_specs=pl.BlockSpec((B, tt, W), lambda i: (0, i, 0)),
        out_shape=jax.ShapeDtypeStruct((B, T, W), BF16),
        scratch_shapes=[pltpu.VMEM((W // LANES, B, LANES), F32),
                        pltpu.VMEM((W // LANES, B * pitch, LANES), F32),
                        pltpu.VMEM((W // LANES, B * pitch, LANES), F32)],
        compiler_params=_cparams("arbitrary"),
        name="rglru",
    )(main, main, _block_diag(w_a).astype(BF16), row(b_a),
      _block_diag(w_x).astype(BF16), row(b_x), row(lam))


def _merge_kernel(s5_ref, at_ref, lr_ref, gate_ref, x_ref, wbr_ref, gb_ref, wout_ref, o_ref):
    dm = x_ref.shape[-1]
    branches = (s5_ref[...], at_ref[...], lr_ref[...])
    merged = None
    for n in range(N_BRANCH):
        gate = _sigmoid(gate_ref[:, n * dm:(n + 1) * dm].astype(F32) + gb_ref[:, n * dm:(n + 1) * dm])
        term = gate * _dot(branches[n], wbr_ref[n])
        merged = term if merged is None else merged + term
    o_ref[...] = x_ref[...] + _dot(merged.astype(BF16), wout_ref[...])


def _merge(s5, attn, lru, gates, x, w_branch, gate_b, w_out, tm):
    B, T, D = x.shape
    W = MIX_WIDTH
    tok = lambda width: pl.BlockSpec((None, tm, width), lambda b, i: (b, i, 0))
    return pl.pallas_call(
        _merge_kernel,
        grid=(B, T // tm),
        in_specs=[tok(W), tok(W), tok(W),
                  tok(N_BRANCH * D), tok(D),
                  _resident((N_BRANCH, W, D)), _resident((1, N_BRANCH * D)), _resident((D, D))],
        out_specs=tok(D),
        out_shape=jax.ShapeDtypeStruct((B, T, D), F32),
        compiler_params=_cparams("parallel", "parallel"),
        name="merge_out",
    )(s5, attn, lru, gates, x, w_branch, gate_b.reshape(1, N_BRANCH * D), w_out)


def _ffn_kernel(x_ref, g_ref, wg_ref, wu_ref, wd_ref, o_ref, *, tf):
    x = x_ref[...]
    h = _rms(x, g_ref[...]).astype(BF16)
    acc = x
    ff = wg_ref.shape[1]
    for c0 in range(0, ff, tf):
        c1 = min(c0 + tf, ff)
        gate = _dot(h, wg_ref[:, c0:c1])
        up = _dot(h, wu_ref[:, c0:c1])
        act = (gate * _sigmoid(gate) * up).astype(BF16)
        acc = acc + _dot(act, wd_ref[c0:c1, :])
    o_ref[...] = acc


def _ffn(x2, g, w_gate, w_up, w_down, tm, tf):
    N, D = x2.shape
    F = w_gate.shape[1]
    return pl.pallas_call(
        functools.partial(_ffn_kernel, tf=tf),
        grid=(N // tm,),
        in_specs=[pl.BlockSpec((tm, D), lambda i: (i, 0)), _resident((1, D)),
                  _resident((D, F)), _resident((D, F)), _resident((F, D))],
        out_specs=pl.BlockSpec((tm, D), lambda i: (i, 0)),
        out_shape=jax.ShapeDtypeStruct((N, D), F32),
        compiler_params=_cparams("parallel"),
        name="ffn_swiglu",
    )(x2, g.reshape(1, D), w_gate, w_up, w_down)


def _router_kernel(x_ref, g_ref, rwh_ref, rwl_ref, rb_ref, h_ref, idx_ref, wgt_ref, *, n_exp):
    h = _rms(x_ref[...], g_ref[...])
    h_hi = h.astype(BF16)
    h_ref[...] = h_hi
    h_lo = (h - h_hi.astype(F32)).astype(BF16)
    logits = _dot(h_hi, rwh_ref[...]) + _dot(h_lo, rwh_ref[...]) + _dot(h_hi, rwl_ref[...])
    lane = lax.broadcasted_iota(jnp.int32, logits.shape, 1)
    lg = jnp.where(lane < n_exp, logits + rb_ref[...], NEG_BIG)
    m1 = jnp.max(lg, axis=1, keepdims=True)
    i1 = jnp.min(jnp.where(lg == m1, lane, LANES), axis=1, keepdims=True)
    lg2 = jnp.where(lane == i1, NEG_BIG, lg)
    m2 = jnp.max(lg2, axis=1, keepdims=True)
    i2 = jnp.min(jnp.where(lg2 == m2, lane, LANES), axis=1, keepdims=True)
    e = jnp.exp(m2 - m1)
    w1 = 1.0 / (1.0 + e)
    w2 = e / (1.0 + e)
    idx_ref[...] = jnp.where(lane == 0, i1, jnp.where(lane == 1, i2, 0)).T[0:idx_ref.shape[0], :]
    wgt_ref[...] = jnp.where(lane == 0, w1, jnp.where(lane == 1, w2, 0.0))


def _router(x2, g, router_w, router_b, tm):
    N, D = x2.shape
    E = router_w.shape[1]
    rw = jnp.zeros((D, LANES), F32).at[:, :E].set(router_w)
    rw_hi = rw.astype(BF16)
    rw_lo = (rw - rw_hi.astype(F32)).astype(BF16)
    rb = jnp.zeros((1, LANES), F32).at[0, :E].set(router_b)
    return pl.pallas_call(
        functools.partial(_router_kernel, n_exp=E),
        grid=(N // tm,),
        in_specs=[pl.BlockSpec((tm, D), lambda i: (i, 0)), _resident((1, D)),
                  _resident((D, LANES)), _resident((D, LANES)), _resident((1, LANES))],
        out_specs=[pl.BlockSpec((tm, D), lambda i: (i, 0)),
                   pl.BlockSpec((SUBLANES, tm), lambda i: (0, i)),
                   pl.BlockSpec((tm, LANES), lambda i: (i, 0))],
        out_shape=[jax.ShapeDtypeStruct((N, D), BF16),
                   jax.ShapeDtypeStruct((SUBLANES, N), jnp.int32),
                   jax.ShapeDtypeStruct((N, LANES), F32)],
        compiler_params=_cparams("parallel"),
        name="moe_router",
    )(x2, g.reshape(1, D), rw_hi, rw_lo, rb)


def _expert_kernel(te_ref, nu_ref, xs_ref, wg_ref, wu_ref, wd_ref, o_ref, acc_ref):
    t = pl.program_id(0)
    f = pl.program_id(1)
    used = t < nu_ref[0]

    @pl.when(f == 0)
    def _():
        acc_ref[...] = jnp.zeros_like(acc_ref)

    @pl.when(used)
    def _():
        xs = xs_ref[...]
        gate = _dot(xs, wg_ref[...].astype(BF16))
        up = _dot(xs, wu_ref[...].astype(BF16))
        act = (gate * _sigmoid(gate) * up).astype(BF16)
        acc_ref[...] += _dot(act, wd_ref[...].astype(BF16))

    @pl.when(f == pl.num_programs(1) - 1)
    def _():
        o_ref[...] = acc_ref[...].astype(BF16)


def _experts(xs, tile_expert, n_used, w_gate, w_up, w_down, j, tm, tf, t0, rows_total, prev):
    S, D = xs.shape
    F = w_gate.shape[3]
    in_specs = [pl.BlockSpec((tm, D), lambda t, f, te, nu: (t, 0)),
                pl.BlockSpec((None, None, D, tf), lambda t, f, te, nu: (j, te[t], 0, f)),
                pl.BlockSpec((None, None, D, tf), lambda t, f, te, nu: (j, te[t], 0, f)),
                pl.BlockSpec((None, None, tf, D), lambda t, f, te, nu: (j, te[t], f, 0))]
    args = (tile_expert, n_used, xs, w_gate, w_up, w_down)
    aliases = {}
    body = _expert_kernel
    if prev is not None:
        in_specs.append(pl.BlockSpec(memory_space=pl.ANY))
        args = args + (prev,)
        aliases = {len(args) - 1: 0}
        body = lambda te, nu, xs_r, wg, wu, wd, prev_r, o, acc: _expert_kernel(te, nu, xs_r, wg, wu, wd, o, acc)
    return pl.pallas_call(
        body,
        grid_spec=pltpu.PrefetchScalarGridSpec(
            num_scalar_prefetch=2,
            grid=(S // tm, F // tf),
            in_specs=in_specs,
            out_specs=pl.BlockSpec((tm, D), lambda t, f, te, nu: (t + t0, 0)),
            scratch_shapes=[pltpu.VMEM((tm, D), F32)]),
        out_shape=jax.ShapeDtypeStruct((rows_total, D), BF16),
        input_output_aliases=aliases,
        compiler_params=_cparams("parallel", "arbitrary"),
        name="moe_experts",
    )(*args)


def _route(idx, n_exp, tm):
    K, N = idx.shape
    experts = jnp.arange(n_exp, dtype=jnp.int32)
    onehot = (idx[:, None, :] == experts[None, :, None]).astype(jnp.int32)
    chosen = jnp.sum(onehot, axis=0)
    before = jnp.cumsum(chosen, axis=1) - chosen
    counts = jnp.sum(chosen, axis=1)
    padded = ((counts + tm - 1) // tm) * tm
    ends = jnp.cumsum(padded)
    starts = ends - padded
    slot = jnp.sum((starts[None, :, None] + before[None]) * onehot, axis=1).reshape(-1)
    S = N * K + n_exp * tm
    tile_start = jnp.arange(S // tm, dtype=jnp.int32) * tm
    last_start = jnp.maximum(ends[-1] - tm, 0)
    tile_expert = jnp.sum((jnp.minimum(tile_start, last_start)[:, None] >= ends[None, :])
                          .astype(jnp.int32), axis=1)
    tile_expert = jnp.minimum(tile_expert, n_exp - 1)
    n_used = (ends[-1] // tm).astype(jnp.int32).reshape(1)
    token = jnp.arange(N, dtype=jnp.int32)
    order = jnp.argsort((idx * N + token[None, :]).reshape(-1)).astype(jnp.int32)
    per_slot = lambda v: jnp.repeat(v[tile_expert], tm)
    within = jnp.arange(S, dtype=jnp.int32) - per_slot(starts)
    compact = jnp.clip(per_slot(jnp.cumsum(counts) - counts) + within, 0, N * K - 1)
    src = jnp.where(within < per_slot(counts), order[compact], jnp.arange(S, dtype=jnp.int32)) % N
    return src, slot.reshape(K, N), tile_expert, n_used


def _combine_kernel(x_ref, y0_ref, y1_ref, wgt_ref, g_ref, o_ref, *, final_norm):
    w0 = wgt_ref[:, 0:1]
    w1 = wgt_ref[:, 1:2]
    x = x_ref[...] + w0 * y0_ref[...].astype(F32) + w1 * y1_ref[...].astype(F32)
    o_ref[...] = _rms(x, g_ref[...]) if final_norm else x


def _combine(x2, y, wgt, g, tm, final_norm):
    N, D = x2.shape
    nt = N // tm
    return pl.pallas_call(
        functools.partial(_combine_kernel, final_norm=final_norm),
        grid=(nt,),
        in_specs=[pl.BlockSpec((tm, D), lambda i: (i, 0)),
                  pl.BlockSpec((tm, D), lambda i: (i, 0)),
                  pl.BlockSpec((tm, D), lambda i: (i + nt, 0)),
                  pl.BlockSpec((tm, LANES), lambda i: (i, 0)),
                  _resident((1, D))],
        out_specs=pl.BlockSpec((tm, D), lambda i: (i, 0)),
        out_shape=jax.ShapeDtypeStruct((N, D), F32),
        compiler_params=_cparams("parallel"),
        name="moe_combine",
    )(x2, y, y, wgt, g.reshape(1, D))


def _moe(x2, g, router_w, router_b, w_gate, w_up, w_down, j, final_g, tm_tok, tm_exp, tf):
    n_exp = router_w.shape[1]
    h, idx, wgt = _router(x2, g, router_w, router_b, tm_tok)
    src, slot, tile_expert, n_used = _route(idx[:TOP_K], n_exp, tm_exp)
    S = src.shape[0]
    nt = S // tm_exp
    cut = (nt // 2) * tm_exp
    ys = None
    for lo, hi in ((0, cut), (cut, S)):
        t0 = lo // tm_exp
        xs = jnp.take(h, src[lo:hi], axis=0, mode="clip")
        ys = _experts(xs, tile_expert[t0:hi // tm_exp], n_used - t0, w_gate, w_up, w_down, j,
                      tm_exp, tf, t0, S, ys)
    y = jnp.take(ys, slot.reshape(-1), axis=0, mode="clip")
    gn = final_g if final_g is not None else jnp.ones((x2.shape[1],), F32)
    return _combine(x2, y, wgt, gn, tm_tok, final_g is not None)


def _final_norm_kernel(x_ref, g_ref, o_ref):
    o_ref[...] = _rms(x_ref[...], g_ref[...])


def _final_norm(x2, g, tm):
    N, D = x2.shape
    return pl.pallas_call(
        _final_norm_kernel,
        grid=(N // tm,),
        in_specs=[pl.BlockSpec((tm, D), lambda i: (i, 0)), _resident((1, D))],
        out_specs=pl.BlockSpec((tm, D), lambda i: (i, 0)),
        out_shape=jax.ShapeDtypeStruct((N, D), F32),
        compiler_params=_cparams("parallel"),
        name="final_norm",
    )(x2, g.reshape(1, D))


def _tile(n, pref):
    t = min(n, pref)
    while n % t:
        t //= 2
    return t


def kernel(x, mix_norm_g, w_in, gate_b, s5_lambda_re, s5_lambda_im, s5_log_dt, s5_b_re, s5_b_im, s5_c_re, s5_c_im, s5_d, s5_w_glu, s5_b_glu, conv_w, conv_b, lru_w_a, lru_b_a, lru_w_x, lru_b_x, lru_lambda, w_branch, w_out, ffn_norm_g, ffn_w_gate, ffn_w_up, ffn_w_down, router_w, router_b, moe_w_gate, moe_w_up, moe_w_down, final_norm_g):
    B, T, D = x.shape
    depth = w_in.shape[0]
    W = MIX_WIDTH
    N = B * T
    tm = _tile(T, 512)
    tq = _tile(T, 256)
    tt = _tile(T, 128)
    L = _tile(T, S5_CHUNK)
    tm_exp = _tile(N, 1024)
    x = x.astype(F32)
    w_in_bf = w_in.astype(BF16)
    for layer in range(depth):
        main, kt, gates = _norm_proj(x, mix_norm_g[layer], w_in_bf, layer, conv_w[layer],
                                     conv_b[layer], tm, tq)
        def s5_with(chunk, rows, layer=layer):
            def run(main):
                consts = _s5_consts(s5_lambda_re[layer], s5_lambda_im[layer], s5_log_dt[layer],
                                    s5_b_re[layer], s5_b_im[layer], s5_c_re[layer],
                                    s5_c_im[layer], chunk)
                return _s5(main, consts, s5_d[layer], s5_w_glu[layer].astype(BF16),
                           s5_b_glu[layer], rows, chunk)
            return run

        step_decay = jnp.max(jnp.abs(s5_lambda_re[layer] * jnp.exp(s5_log_dt[layer])[:, None]))
        Ls = _tile(T, S5_SMALL_CHUNK)
        s5 = lax.cond(step_decay * (L // 2) < S5_SAFE_EXPONENT,
                      s5_with(L, tm), s5_with(Ls, _tile(T, 8 * Ls)), main)
        attn = _attention(main, kt, tq)
        lru = _rglru(main, lru_w_a[layer], lru_b_a[layer], lru_w_x[layer], lru_b_x[layer],
                     lru_lambda[layer], tt)
        x = _merge(s5, attn, lru, gates, x, w_branch[layer].astype(BF16),
                   gate_b[layer], w_out[layer].astype(BF16), tm)
        x2 = x.reshape(N, D)
        last = layer == depth - 1
        j = layer // 2
        if layer % 2 == 0:
            x2 = _ffn(x2, ffn_norm_g[layer], ffn_w_gate[j].astype(BF16), ffn_w_up[j].astype(BF16),
                      ffn_w_down[j].astype(BF16), tm, 3 * MXU_TILE)
            if last:
                x2 = _final_norm(x2, final_norm_g, tm)
        else:
            x2 = _moe(x2, ffn_norm_g[layer], router_w[j], router_b[j], moe_w_gate, moe_w_up,
                      moe_w_down, j, final_norm_g if last else None, tm, tm_exp,
                      _tile(moe_w_gate.shape[3], 2 * MXU_TILE))
        x = x2.reshape(B, T, D)
    return x
```
